```python
import jax, jax.numpy as jnp
from jax import lax
import numpy as np

D_MODEL = 2048
BATCH = 1
SEQ = 16384
DEPTH = 4
DEC_BATCH = 8
DEC_SEQ = 64
PAST_LEN = 4096

CHUNK = 64
D_MIX = D_MODEL
GLA_WIDTH = D_MIX // 2
GLA_HEADS = 4
GLA_DV = GLA_WIDTH // GLA_HEADS
GLA_DK = GLA_DV // 2
GLA_KEY_WIDTH = GLA_HEADS * GLA_DK
GATE_RANK = 16
GATE_TEMP = 16.0
MLP_WIDTH = D_MIX - GLA_WIDTH
MLP_GROUPS = 4
MLP_GC = MLP_WIDTH // MLP_GROUPS
MLP_CHUNK = 128
D_IN = 2 * GLA_KEY_WIDTH + 2 * GLA_WIDTH + GATE_RANK + 3 * MLP_WIDTH
EPS = 1e-6

kernel_name = "hymba_gla_gmlp_streaming_step"


def rmsnorm(x, g):
    xf = x.astype(jnp.float32)
    y = xf * lax.rsqrt(jnp.mean(xf * xf, axis=-1, keepdims=True) + EPS)
    return (y * g.astype(jnp.float32)).astype(x.dtype)


def layernorm(x, g, b):
    xf = x.astype(jnp.float32)
    mu = jnp.mean(xf, axis=-1, keepdims=True)
    xc = xf - mu
    y = xc * lax.rsqrt(jnp.mean(xc * xc, axis=-1, keepdims=True) + EPS)
    return (y * g.astype(jnp.float32) + b.astype(jnp.float32)).astype(x.dtype)


def split_projection(z):
    sizes = [GLA_KEY_WIDTH, GLA_KEY_WIDTH, GLA_WIDTH, GLA_WIDTH, GATE_RANK, MLP_WIDTH, MLP_WIDTH, MLP_WIDTH]
    outs, off = [], 0
    for s in sizes:
        outs.append(z[..., off:off + s])
        off += s
    return outs


def gla_scan(q, k, v, log_a, s0, blk):
    B, L, H, _ = q.shape
    n = L // blk

    def to_blocks(t):
        return t.astype(jnp.float32).reshape(B, n, blk, H, t.shape[-1]).transpose(1, 0, 3, 2, 4)

    causal = jnp.tril(jnp.ones((blk, blk), dtype=bool))[:, :, None]

    def step(S, inp):
        qb, kb, vb, gb = inp
        b = jnp.cumsum(gb, axis=-2)
        diff = b[..., :, None, :] - b[..., None, :, :]
        decay = jnp.where(causal, jnp.exp(jnp.where(causal, diff, 0.0)), 0.0)
        scores = jnp.einsum('bhtd,bhsd,bhtsd->bhts', qb, kb, decay)
        o = (jnp.einsum('bhts,bhsv->bhtv', scores, vb)
             + jnp.einsum('bhtd,bhdv->bhtv', qb * jnp.exp(b), S))
        b_last = b[..., -1:, :]
        S_new = (jnp.exp(b_last[..., 0, :])[..., None] * S
                 + jnp.einsum('bhsd,bhsv->bhdv', kb * jnp.exp(b_last - b), vb))
        return S_new, o

    S, o = lax.scan(step, s0, (to_blocks(q), to_blocks(k), to_blocks(v), to_blocks(log_a)))
    o = o.transpose(1, 0, 3, 2, 4).reshape(B, L, H, v.shape[-1])
    return o, S


def spatial_gate(v, w_s, b_s, blk):
    B, L, G, C = v.shape
    n = L // blk
    wm = jnp.where(jnp.tril(jnp.ones((blk, blk), dtype=bool))[None], w_s[:, :blk, :blk], 0.0)
    vb = v.reshape(B, n, blk, G, C)
    s = jnp.einsum('gts,bnsgc->bntgc', wm.astype(v.dtype), vb) + b_s[:, :blk].T[None, None, :, :, None]
    return s.reshape(B, L, G, C)


def layer(x, s0, gla_blk, w_in, w_gate_up, b_gate, w_s, b_s, norm_g, gla_norm_g, mlp_ln_g, mlp_ln_b, w_out):
    B, L, _ = x.shape
    h = rmsnorm(x, norm_g)
    z = jnp.einsum('bld,de->ble', h, w_in)
    q, k, v, g_a, lr, u, vm, g_b = split_projection(z)
    log_a = jax.nn.log_sigmoid((jnp.einsum('blr,rk->blk', lr, w_gate_up) + b_gate).astype(jnp.float32)) / GATE_TEMP
    q = (q * (GLA_DK ** -0.5)).reshape(B, L, GLA_HEADS, GLA_DK)
    k = k.reshape(B, L, GLA_HEADS, GLA_DK)
    v = v.reshape(B, L, GLA_HEADS, GLA_DV)
    log_a = log_a.reshape(B, L, GLA_HEADS, GLA_DK)
    o, S = gla_scan(q, k, v, log_a, s0, gla_blk)
    o_a = rmsnorm(o.astype(x.dtype), gla_norm_g).reshape(B, L, GLA_WIDTH) * jax.nn.silu(g_a)
    vm_n = layernorm(vm.reshape(B, L, MLP_GROUPS, MLP_GC),
                     mlp_ln_g.reshape(MLP_GROUPS, MLP_GC), mlp_ln_b.reshape(MLP_GROUPS, MLP_GC))
    sg = spatial_gate(vm_n, w_s, b_s, min(L, MLP_CHUNK)).reshape(B, L, MLP_WIDTH)
    o_b = u * sg * jax.nn.silu(g_b)
    y = x + jnp.einsum('ble,ed->bld', jnp.concatenate([o_a, o_b], axis=-1), w_out)
    return y, S, vm_n.reshape(B, L, MLP_WIDTH)


def setup_inputs(seed: int = 0) -> dict:
    key = jax.random.key(seed)
    ks = jax.random.split(key, 16)
    f32 = jnp.float32
    return {
        "x_prompt": jax.random.normal(ks[0], (BATCH, SEQ, D_MODEL), f32),
        "x_sample": jax.random.normal(ks[1], (DEC_BATCH, DEC_SEQ, D_MODEL), f32),
        "state_gla": 0.5 * jax.random.normal(ks[2], (DEPTH, DEC_BATCH, GLA_HEADS, GLA_DK, GLA_DV), f32),
        "w_in": jax.random.normal(ks[3], (DEPTH, D_MODEL, D_IN), f32) * D_MODEL ** -0.5,
        "w_gate_up": jax.random.normal(ks[4], (DEPTH, GATE_RANK, GLA_KEY_WIDTH), f32) * GATE_RANK ** -0.5,
        "b_gate": 0.1 * jax.random.normal(ks[5], (DEPTH, GLA_KEY_WIDTH), f32),
        "w_s": jax.random.normal(ks[6], (DEPTH, MLP_GROUPS, MLP_CHUNK, MLP_CHUNK), f32) * MLP_CHUNK ** -0.5,
        "b_s": 1.0 + 0.1 * jax.random.normal(ks[7], (DEPTH, MLP_GROUPS, MLP_CHUNK), f32),
        "norm_g": 1.0 + 0.05 * jax.random.normal(ks[8], (DEPTH, D_MODEL), f32),
        "gla_norm_g": 1.0 + 0.05 * jax.random.normal(ks[9], (DEPTH, GLA_DV), f32),
        "mlp_ln_g": 1.0 + 0.05 * jax.random.normal(ks[10], (DEPTH, MLP_WIDTH), f32),
        "mlp_ln_b": 0.05 * jax.random.normal(ks[11], (DEPTH, MLP_WIDTH), f32),
        "w_out": jax.random.normal(ks[12], (DEPTH, D_MIX, D_MODEL), f32) * D_MIX ** -0.5,
        "final_norm_g": 1.0 + 0.05 * jax.random.normal(ks[13], (D_MODEL,), f32),
    }


def reference(x_prompt, x_sample, state_gla, w_in, w_gate_up, b_gate, w_s, b_s, norm_g, gla_norm_g,
              mlp_ln_g, mlp_ln_b, w_out, final_norm_g):
    yp, ys = x_prompt, x_sample
    gla_p, gla_s, v_s = [], [], []
    for l in range(DEPTH):
        params = (w_in[l], w_gate_up[l], b_gate[l], w_s[l], b_s[l], norm_g[l], gla_norm_g[l],
                  mlp_ln_g[l], mlp_ln_b[l], w_out[l])
        s0_p = jnp.zeros((yp.shape[0], GLA_HEADS, GLA_DK, GLA_DV), jnp.float32)
        yp, Sp, _ = layer(yp, s0_p, CHUNK, *params)
        ys, Ss, vs = layer(ys, state_gla[l].astype(jnp.float32), ys.shape[1], *params)
        gla_p.append(Sp.astype(x_prompt.dtype))
        gla_s.append(Ss.astype(state_gla.dtype))
        v_s.append(vs)
    y_prompt = rmsnorm(yp, final_norm_g)
    y_sample = rmsnorm(ys, final_norm_g)
    gla_state_prompt = jnp.stack(gla_p)
    gla_state_sample = jnp.stack(gla_s)
    mlp_v_sample = jnp.stack(v_s)
    return (y_prompt, y_sample, gla_state_prompt, gla_state_sample, mlp_v_sample)
```

```python
import functools

import jax
import jax.numpy as jnp
from jax import lax
from jax.experimental import pallas as pl
from jax.experimental.pallas import tpu as pltpu

D_MODEL = 2048
DEPTH = 4
GLA_HEADS = 4
GLA_DK = 128
GLA_DV = 256
GLA_KEY_WIDTH = GLA_HEADS * GLA_DK
GLA_WIDTH = GLA_HEADS * GLA_DV
GATE_RANK = 16
GATE_TEMP = 16.0
MLP_GROUPS = 4
MLP_GC = 256
MLP_WIDTH = MLP_GROUPS * MLP_GC
MLP_CHUNK = 128
EPS = 1e-6

Z_Q = 0
Z_K = Z_Q + GLA_KEY_WIDTH
Z_V = Z_K + GLA_KEY_WIDTH
Z_GA = Z_V + GLA_WIDTH
Z_LR = Z_GA + GLA_WIDTH
Z_U = Z_GA + GLA_WIDTH
Z_VM = Z_U + MLP_WIDTH
Z_GB = Z_VM + MLP_WIDTH
Z_MAIN = Z_GB + MLP_WIDTH

LANES = 128
V7X_VMEM_BYTES = 64 * 1024 * 1024
VMEM_INTERNAL_BYTES = 12 * 1024 * 1024

F32 = jnp.float32
BF16 = jnp.bfloat16


def _vmem_limit(block_bytes, scratch_bytes=0):
    need = 2 * block_bytes + scratch_bytes + VMEM_INTERNAL_BYTES
    return min(need, V7X_VMEM_BYTES - 8 * 1024 * 1024)


def _dot(a, b):
    return jnp.dot(a, b, preferred_element_type=F32)


def _dot_nt(a, b):
    return lax.dot_general(a, b, (((1,), (1,)), ((), ())), preferred_element_type=F32)


def _dot_tn(a, b):
    return lax.dot_general(a, b, (((0,), (0,)), ((), ())), preferred_element_type=F32)


def _split_bf16(x, parts):
    out = []
    r = x
    for _ in range(parts - 1):
        p = r.astype(BF16)
        out.append(p)
        r = r - p.astype(F32)
    out.append(r.astype(BF16))
    return out


def _silu(x):
    return x / (1.0 + jnp.exp(-x))


def _proj_in_kernel(x_ref, ng_ref, wlr_ref, wg_ref, bg_ref, w_ref, z_ref, la_ref, h_scr, *, rows_per_pass):
    tm = x_ref.shape[0]

    @pl.when(pl.program_id(1) == 0)
    def _():
        wg_hi, wg_lo = _split_bf16(wg_ref[...], 2)

        def norm_rows(i, carry):
            rows = pl.ds(pl.multiple_of(i * rows_per_pass, rows_per_pass), rows_per_pass)
            x = x_ref[rows, :]
            ms = jnp.mean(x * x, axis=-1, keepdims=True)
            h = (x * lax.rsqrt(ms + EPS) * ng_ref[...]).astype(BF16)
            h_scr[rows, :] = h
            lr_hi, lr_lo = _split_bf16(_dot(h, wlr_ref[...]), 2)
            xg = _dot(lr_hi, wg_hi) + _dot(lr_lo, wg_hi) + _dot(lr_hi, wg_lo) + bg_ref[...]
            log_sig = jnp.minimum(xg, 0.0) - jnp.log(1.0 + jnp.exp(-jnp.abs(xg)))
            la_ref[rows, :] = log_sig * (1.0 / GATE_TEMP)
            return carry

        lax.fori_loop(0, tm // rows_per_pass, norm_rows, 0)

    z_ref[...] = _dot(h_scr[...], w_ref[...]).astype(BF16)


def _proj_in(x2d, norm_g, w_lr, w_g, b_g, w_main):
    rows = x2d.shape[0]
    tm = min(1024, rows)
    tn = 1024
    rows_per_pass = min(256, tm)
    blocks = tm * D_MODEL * 4 + D_MODEL * tn * 2 + tm * tn * 2 + tm * GLA_KEY_WIDTH * 4 + D_MODEL * LANES * 2
    return pl.pallas_call(
        functools.partial(_proj_in_kernel, rows_per_pass=rows_per_pass),
        grid=(rows // tm, Z_MAIN // tn),
        in_specs=[
            pl.BlockSpec((tm, D_MODEL), lambda i, j: (i, 0)),
            pl.BlockSpec((1, D_MODEL), lambda i, j: (0, 0)),
            pl.BlockSpec((D_MODEL, LANES), lambda i, j: (0, 0)),
            pl.BlockSpec((LANES, GLA_KEY_WIDTH), lambda i, j: (0, 0)),
            pl.BlockSpec((1, GLA_KEY_WIDTH), lambda i, j: (0, 0)),
            pl.BlockSpec((D_MODEL, tn), lambda i, j: (0, j)),
        ],
        out_specs=[
            pl.BlockSpec((tm, tn), lambda i, j: (i, j)),
            pl.BlockSpec((tm, GLA_KEY_WIDTH), lambda i, j: (i, 0)),
        ],
        out_shape=[
            jax.ShapeDtypeStruct((rows, Z_MAIN), BF16),
            jax.ShapeDtypeStruct((rows, GLA_KEY_WIDTH), F32),
        ],
        scratch_shapes=[pltpu.VMEM((tm, D_MODEL), BF16)],
        compiler_params=pltpu.CompilerParams(
            dimension_semantics=("arbitrary", "arbitrary"),
            vmem_limit_bytes=_vmem_limit(blocks, tm * D_MODEL * 2)),
        name="proj_in",
    )(x2d, norm_g, w_lr, w_g, b_g, w_main)


def _mix_kernel(z_ref, la_ref, s0_ref, gng_ref, lng_ref, lnb_ref, ws_ref, bs_ref, *rest, chunk, emit_vn):
    if emit_vn:
        o_ref, sfin_ref, vn_ref, st_scr = rest
    else:
        o_ref, sfin_ref, st_scr = rest
        vn_ref = None
    c = chunk
    n_levels = c.bit_length() - 1
    n_chunks = z_ref.shape[1] // c
    t_idx = pl.program_id(1)

    @pl.when(t_idx == 0)
    def _():
        for h in range(GLA_HEADS):
            st_scr[h] = s0_ref[0, h].T

    rt = lax.broadcasted_iota(jnp.int32, (c, c), 0)
    cs = lax.broadcasted_iota(jnp.int32, (c, c), 1)
    differ = rt ^ cs
    top_bit = jnp.zeros((c, c), jnp.int32)
    for j in range(n_levels):
        top_bit = top_bit + jnp.where(differ >= (1 << j), 1, 0)
    level = jnp.where(rt >= cs, top_bit, -1)
    tri_ones = jnp.where(rt >= cs, 1.0, 0.0).astype(BF16)
    w_tril = [jnp.where(rt >= cs, ws_ref[g], 0.0).astype(BF16) for g in range(MLP_GROUPS)]
    row_id = lax.broadcasted_iota(jnp.int32, (c, GLA_DK), 0)
    row_bit = [((row_id >> j) & 1) == 1 for j in range(n_levels)]
    q_scale = GLA_DK ** -0.5

    def chunk_body(ci, carry):
        rows = pl.ds(pl.multiple_of(ci * c, c), c)

        la = la_ref[0, rows, :]
        b_all = sum(_dot(tri_ones, part) for part in _split_bf16(la, 3))
        for h in range(GLA_HEADS):
            ks = slice(h * GLA_DK, (h + 1) * GLA_DK)
            vs = slice(h * GLA_DV, (h + 1) * GLA_DV)
            b = b_all[:, ks]
            q = z_ref[0, rows, Z_Q + h * GLA_DK:Z_Q + (h + 1) * GLA_DK].astype(F32) * q_scale
            k = z_ref[0, rows, Z_K + h * GLA_DK:Z_K + (h + 1) * GLA_DK].astype(F32)
            v = z_ref[0, rows, Z_V + h * GLA_DV:Z_V + (h + 1) * GLA_DV]
            scores = jnp.where(level == 0, _dot_nt(q.astype(BF16), k.astype(BF16)), 0.0)
            f = b
            for j in range(n_levels):
                half = 1 << j
                f_prev = pltpu.roll(f, half, 0)
                decay = jnp.exp(jnp.where(row_bit[j], b - f_prev, f - b))
                p = _dot_nt((q * decay).astype(BF16), (k * decay).astype(BF16))
                scores = jnp.where(level == j + 1, p, scores)
                f = jnp.where(row_bit[j], f, pltpu.roll(f, c - half, 0))
            b_last = f
            st = st_scr[h]
            o = _dot(scores.astype(BF16), v) + _dot_nt((q * jnp.exp(b)).astype(BF16), st.astype(BF16))
            k_dec = (k * jnp.exp(b_last - b)).astype(BF16)
            st_scr[h] = jnp.exp(b_last[0:1, :]) * st + _dot_tn(v, k_dec)
            ms = jnp.mean(o * o, axis=-1, keepdims=True)
            g_a = z_ref[0, rows, Z_GA + h * GLA_DV:Z_GA + (h + 1) * GLA_DV].astype(F32)
            o_a = o * lax.rsqrt(ms + EPS) * gng_ref[...] * _silu(g_a)
            o_ref[0, rows, vs] = o_a.astype(o_ref.dtype)

        for g in range(MLP_GROUPS):
            gs = slice(g * MLP_GC, (g + 1) * MLP_GC)
            vm = z_ref[0, rows, Z_VM + g * MLP_GC:Z_VM + (g + 1) * MLP_GC].astype(F32)
            mu = jnp.mean(vm, axis=-1, keepdims=True)
            xc = vm - mu
            var = jnp.mean(xc * xc, axis=-1, keepdims=True)
            vn = xc * lax.rsqrt(var + EPS) * lng_ref[:, gs] + lnb_ref[:, gs]
            if vn_ref is not None:
                vn_ref[0, rows, gs] = vn
            sg = _dot(w_tril[g], vn.astype(BF16)) + bs_ref[g]
            u = z_ref[0, rows, Z_U + g * MLP_GC:Z_U + (g + 1) * MLP_GC].astype(F32)
            g_b = z_ref[0, rows, Z_GB + g * MLP_GC:Z_GB + (g + 1) * MLP_GC].astype(F32)
            o_b = u * sg * _silu(g_b)
            o_ref[0, rows, GLA_WIDTH + g * MLP_GC:GLA_WIDTH + (g + 1) * MLP_GC] = o_b.astype(o_ref.dtype)
        return carry

    lax.fori_loop(0, n_chunks, chunk_body, 0)

    @pl.when(t_idx == pl.num_programs(1) - 1)
    def _():
        for h in range(GLA_HEADS):
            sfin_ref[0, h] = st_scr[h].T


def _mix(z, la, s0, gla_norm_g, ln_g, ln_b, w_s, b_s, *, chunk, emit_vn):
    nb, seq, _ = z.shape
    t = min(512, seq)
    blocks = (t * Z_MAIN * 2 + t * GLA_KEY_WIDTH * 4 + 2 * GLA_HEADS * GLA_DK * GLA_DV * 4
              + t * D_MODEL * 2 + MLP_GROUPS * chunk * (chunk + LANES) * 4)
    out_specs = [
        pl.BlockSpec((1, t, D_MODEL), lambda b, i: (b, i, 0)),
        pl.BlockSpec((1, GLA_HEADS, GLA_DK, GLA_DV), lambda b, i: (b, 0, 0, 0)),
    ]
    out_shape = [
        jax.ShapeDtypeStruct((nb, seq, D_MODEL), BF16),
        jax.ShapeDtypeStruct((nb, GLA_HEADS, GLA_DK, GLA_DV), F32),
    ]
    if emit_vn:
        out_specs.append(pl.BlockSpec((1, t, MLP_WIDTH), lambda b, i: (b, i, 0)))
        out_shape.append(jax.ShapeDtypeStruct((nb, seq, MLP_WIDTH), F32))
        blocks += t * MLP_WIDTH * 4
    return pl.pallas_call(
        functools.partial(_mix_kernel, chunk=chunk, emit_vn=emit_vn),
        grid=(nb, seq // t),
        in_specs=[
            pl.BlockSpec((1, t, Z_MAIN), lambda b, i: (b, i, 0)),
            pl.BlockSpec((1, t, GLA_KEY_WIDTH), lambda b, i: (b, i, 0)),
            pl.BlockSpec((1, GLA_HEADS, GLA_DK, GLA_DV), lambda b, i: (b, 0, 0, 0)),
            pl.BlockSpec((1, GLA_DV), lambda b, i: (0, 0)),
            pl.BlockSpec((1, MLP_WIDTH), lambda b, i: (0, 0)),
            pl.BlockSpec((1, MLP_WIDTH), lambda b, i: (0, 0)),
            pl.BlockSpec((MLP_GROUPS, chunk, chunk), lambda b, i: (0, 0, 0)),
            pl.BlockSpec((MLP_GROUPS, chunk, 1), lambda b, i: (0, 0, 0)),
        ],
        out_specs=out_specs,
        out_shape=out_shape,
        scratch_shapes=[pltpu.VMEM((GLA_HEADS, GLA_DV, GLA_DK), F32)],
        compiler_params=pltpu.CompilerParams(
            dimension_semantics=("arbitrary", "arbitrary"),
            vmem_limit_bytes=_vmem_limit(blocks, GLA_HEADS * GLA_DV * GLA_DK * 4)),
        name="mix",
    )(z, la, s0, gla_norm_g, ln_g, ln_b, w_s, b_s)


def _proj_out_kernel(o_ref, x_ref, w_ref, fg_ref, y_ref, *, final_norm):
    y = x_ref[...] + _dot(o_ref[...], w_ref[...])
    if final_norm:
        ms = jnp.mean(y * y, axis=-1, keepdims=True)
        y = y * lax.rsqrt(ms + EPS) * fg_ref[...]
    y_ref[...] = y


def _proj_out(o2d, x2d, w_out, final_g, *, final_norm):
    rows = x2d.shape[0]
    tm = min(512, rows)
    blocks = tm * D_MODEL * (2 + 4 + 4) + D_MODEL * D_MODEL * 2
    return pl.pallas_call(
        functools.partial(_proj_out_kernel, final_norm=final_norm),
        grid=(rows // tm,),
        in_specs=[
            pl.BlockSpec((tm, D_MODEL), lambda i: (i, 0)),
            pl.BlockSpec((tm, D_MODEL), lambda i: (i, 0)),
            pl.BlockSpec((D_MODEL, D_MODEL), lambda i: (0, 0)),
            pl.BlockSpec((1, D_MODEL), lambda i: (0, 0)),
        ],
        out_specs=pl.BlockSpec((tm, D_MODEL), lambda i: (i, 0)),
        out_shape=jax.ShapeDtypeStruct((rows, D_MODEL), F32),
        compiler_params=pltpu.CompilerParams(
            dimension_semantics=("arbitrary",),
            vmem_limit_bytes=_vmem_limit(blocks)),
        name="proj_out",
    )(o2d, x2d, w_out, final_g)


def _layer(x, s0, p, *, emit_vn, final_norm):
    nb, seq, _ = x.shape
    chunk = min(seq, MLP_CHUNK)
    x2d = x.reshape(nb * seq, D_MODEL)
    z, la = _proj_in(x2d, p["norm_g"], p["w_lr"], p["w_g"], p["b_g"], p["w_main"])
    outs = _mix(z.reshape(nb, seq, Z_MAIN), la.reshape(nb, seq, GLA_KEY_WIDTH), s0,
                p["gla_norm_g"], p["ln_g"], p["ln_b"],
                p["w_s"][:, :chunk, :chunk], p["b_s"][:, :chunk, None],
                chunk=chunk, emit_vn=emit_vn)
    o, s_fin = outs[0], outs[1]
    y = _proj_out(o.reshape(nb * seq, D_MODEL), x2d, p["w_out"], p["final_g"], final_norm=final_norm)
    return y.reshape(nb, seq, D_MODEL), s_fin, (outs[2] if emit_vn else None)


def kernel(x_prompt, x_sample, state_gla, w_in, w_gate_up, b_gate, w_s, b_s, norm_g, gla_norm_g,
           mlp_ln_g, mlp_ln_b, w_out, final_norm_g):
    w_main = jnp.concatenate([w_in[:, :, :Z_LR], w_in[:, :, Z_LR + GATE_RANK:]], axis=-1).astype(BF16)
    w_lr = jnp.pad(w_in[:, :, Z_LR:Z_LR + GATE_RANK], ((0, 0), (0, 0), (0, LANES - GATE_RANK))).astype(BF16)
    w_g = jnp.pad(w_gate_up, ((0, 0), (0, LANES - GATE_RANK), (0, 0)))
    w_out_b = w_out.astype(BF16)

    yp, ys = x_prompt, x_sample
    gla_p, gla_s, v_s = [], [], []
    for l in range(DEPTH):
        p = dict(norm_g=norm_g[l][None], w_lr=w_lr[l], w_g=w_g[l], b_g=b_gate[l][None], w_main=w_main[l],
                 gla_norm_g=gla_norm_g[l][None], ln_g=mlp_ln_g[l][None], ln_b=mlp_ln_b[l][None],
                 w_s=w_s[l], b_s=b_s[l], w_out=w_out_b[l], final_g=final_norm_g[None])
        last = l == DEPTH - 1
        s0_p = jnp.zeros((yp.shape[0], GLA_HEADS, GLA_DK, GLA_DV), F32)
        yp, sp, _ = _layer(yp, s0_p, p, emit_vn=False, final_norm=last)
        ys, ss, vs = _layer(ys, state_gla[l], p, emit_vn=True, final_norm=last)
        gla_p.append(sp)
        gla_s.append(ss)
        v_s.append(vs)
    return (yp, ys, jnp.stack(gla_p), jnp.stack(gla_s), jnp.stack(v_s))
```

```python
import functools

import jax
import jax.numpy as jnp
from jax import lax
from jax.experimental import pallas as pl
from jax.experimental.pallas import tpu as pltpu

D_MODEL = 2048
DEPTH = 4
GLA_HEADS = 4
GLA_DK = 128
GLA_DV = 256
GLA_KEY_WIDTH = GLA_HEADS * GLA_DK
GLA_WIDTH = GLA_HEADS * GLA_DV
GATE_RANK = 16
GATE_TEMP = 16.0
MLP_GROUPS = 4
MLP_GC = 256
MLP_WIDTH = MLP_GROUPS * MLP_GC
MLP_CHUNK = 128
EPS = 1e-6

Z_Q = 0
Z_K = Z_Q + GLA_KEY_WIDTH
Z_V = Z_K + GLA_KEY_WIDTH
Z_GA = Z_V + GLA_WIDTH
Z_LR = Z_GA + GLA_WIDTH
Z_U = Z_GA + GLA_WIDTH
Z_VM = Z_U + MLP_WIDTH
Z_GB = Z_VM + MLP_WIDTH
Z_MAIN = Z_GB + MLP_WIDTH
Z_HALF = Z_MAIN // 2

LANES = 128
V7X_VMEM_BYTES = 64 * 1024 * 1024
VMEM_INTERNAL_BYTES = 12 * 1024 * 1024

F32 = jnp.float32
BF16 = jnp.bfloat16


def _vmem_limit(pipelined_bytes, resident_bytes=0):
    need = 2 * pipelined_bytes + resident_bytes + VMEM_INTERNAL_BYTES
    return min(need, V7X_VMEM_BYTES - 6 * 1024 * 1024)


def _dot(a, b):
    return jnp.dot(a, b, preferred_element_type=F32)


def _dot_nt(a, b):
    return lax.dot_general(a, b, (((1,), (1,)), ((), ())), preferred_element_type=F32)


def _dot_tn(a, b):
    return lax.dot_general(a, b, (((0,), (0,)), ((), ())), preferred_element_type=F32)


def _split_bf16(x, parts):
    out = []
    r = x
    for _ in range(parts - 1):
        p = r.astype(BF16)
        out.append(p)
        r = r - p.astype(F32)
    out.append(r.astype(BF16))
    return out


def _silu(x):
    hx = 0.5 * x
    return hx + hx * jnp.tanh(hx)


def _rmsnorm(x, g):
    ms = jnp.mean(x * x, axis=-1, keepdims=True)
    return x * lax.rsqrt(ms + EPS) * g


def _resident(block_shape, index_map):
    return pl.BlockSpec(block_shape, index_map, pipeline_mode=pl.Buffered(1))


def _norm_in_kernel(x_ref, g_ref, h_ref):
    h_ref[...] = _rmsnorm(x_ref[...], g_ref[...]).astype(h_ref.dtype)


def _norm_in(x2d, g):
    rows = x2d.shape[0]
    tm = min(512, rows)
    return pl.pallas_call(
        _norm_in_kernel,
        grid=(rows // tm,),
        in_specs=[pl.BlockSpec((tm, D_MODEL), lambda i: (i, 0)),
                  pl.BlockSpec((1, D_MODEL), lambda i: (0, 0))],
        out_specs=pl.BlockSpec((tm, D_MODEL), lambda i: (i, 0)),
        out_shape=jax.ShapeDtypeStruct((rows, D_MODEL), BF16),
        compiler_params=pltpu.CompilerParams(
            dimension_semantics=("arbitrary",),
            vmem_limit_bytes=_vmem_limit(tm * D_MODEL * 6)),
        name="norm_in",
    )(x2d, g)


def _layer_in_kernel(h_ref, wa_ref, wb_ref, wlr_ref, wg_ref, bg_ref, s0_ref, gng_ref, lng_ref, lnb_ref,
                     ws_ref, bs_ref, *rest, chunk, emit_vn, n_cols):
    if emit_vn:
        o_ref, sfin_ref, vn_ref, z_scr, la_scr, st_scr = rest
    else:
        o_ref, sfin_ref, z_scr, la_scr, st_scr = rest
        vn_ref = None
    c = chunk
    n_levels = c.bit_length() - 1
    n_chunks = h_ref.shape[1] // c
    t_idx = pl.program_id(1)

    @pl.when(t_idx == 0)
    def _():
        for h in range(GLA_HEADS):
            st_scr[h] = s0_ref[0, h].T

    hx = h_ref[0]
    for w_ref, base in ((wa_ref, 0), (wb_ref, Z_HALF)):
        for n in range(Z_HALF // n_cols):
            cols = slice(n * n_cols, (n + 1) * n_cols)
            z_scr[:, base + n * n_cols:base + (n + 1) * n_cols] = _dot(hx, w_ref[:, cols]).astype(BF16)
    wg_hi, wg_lo = _split_bf16(wg_ref[...], 2)
    lr_hi, lr_lo = _split_bf16(_dot(hx, wlr_ref[...]), 2)
    xg = _dot(lr_hi, wg_hi) + _dot(lr_lo, wg_hi) + _dot(lr_hi, wg_lo) + bg_ref[...]
    log_sig = jnp.minimum(xg, 0.0) - jnp.log(1.0 + jnp.exp(-jnp.abs(xg)))
    la_scr[...] = log_sig * (1.0 / GATE_TEMP)

    rt = lax.broadcasted_iota(jnp.int32, (c, c), 0)
    cs = lax.broadcasted_iota(jnp.int32, (c, c), 1)
    differ = rt ^ cs
    top_bit = jnp.zeros((c, c), jnp.int32)
    for j in range(n_levels):
        top_bit = top_bit + jnp.where(differ >= (1 << j), 1, 0)
    level = jnp.where(rt >= cs, top_bit, -1)
    tri_ones = jnp.where(rt >= cs, 1.0, 0.0).astype(BF16)
    w_tril = [jnp.where(rt >= cs, ws_ref[g], 0.0).astype(BF16) for g in range(MLP_GROUPS)]
    row_id = lax.broadcasted_iota(jnp.int32, (c, GLA_DK), 0)
    row_bit = [((row_id >> j) & 1) == 1 for j in range(n_levels)]
    q_scale = GLA_DK ** -0.5

    for ci in range(n_chunks):
        rows = slice(ci * c, (ci + 1) * c)

        b_all = sum(_dot(tri_ones, part) for part in _split_bf16(la_scr[rows, :], 2))
        for h in range(GLA_HEADS):
            vs = slice(h * GLA_DV, (h + 1) * GLA_DV)
            b = b_all[:, h * GLA_DK:(h + 1) * GLA_DK]
            q = z_scr[rows, Z_Q + h * GLA_DK:Z_Q + (h + 1) * GLA_DK]
            k = z_scr[rows, Z_K + h * GLA_DK:Z_K + (h + 1) * GLA_DK]
            v = z_scr[rows, Z_V + h * GLA_DV:Z_V + (h + 1) * GLA_DV]
            scores = jnp.where(level == 0, _dot_nt(q, k), 0.0)
            f = b
            for j in range(n_levels):
                half = 1 << j
                f_prev = pltpu.roll(f, half, 0)
                decay = jnp.exp(jnp.where(row_bit[j], b - f_prev, f - b)).astype(BF16)
                p = _dot_nt(q * decay, k * decay)
                scores = jnp.where(level == j + 1, p, scores)
                f = jnp.where(row_bit[j], f, pltpu.roll(f, c - half, 0))
            b_last = f
            st = st_scr[h]
            o = _dot(scores.astype(BF16), v) + _dot_nt(q * jnp.exp(b).astype(BF16), st.astype(BF16))
            k_dec = k * jnp.exp(b_last - b).astype(BF16)
            st_scr[h] = jnp.exp(b_last[0:1, :]) * st + _dot_tn(v, k_dec)
            ms = jnp.mean(o * o, axis=-1, keepdims=True)
            inv = q_scale * lax.rsqrt(q_scale * q_scale * ms + EPS)
            g_a = z_scr[rows, Z_GA + h * GLA_DV:Z_GA + (h + 1) * GLA_DV].astype(F32)
            o_a = o * inv * gng_ref[...] * _silu(g_a)
            o_ref[0, rows, vs] = o_a.astype(o_ref.dtype)

        for g in range(MLP_GROUPS):
            gs = slice(g * MLP_GC, (g + 1) * MLP_GC)
            vm = z_scr[rows, Z_VM + g * MLP_GC:Z_VM + (g + 1) * MLP_GC].astype(F32)
            mu = jnp.mean(vm, axis=-1, keepdims=True)
            xc = vm - mu
            var = jnp.mean(xc * xc, axis=-1, keepdims=True)
            vn = xc * lax.rsqrt(var + EPS) * lng_ref[:, gs] + lnb_ref[:, gs]
            if vn_ref is not None:
                vn_ref[0, rows, gs] = vn
            sg = _dot(w_tril[g], vn.astype(BF16)) + bs_ref[g]
            u = z_scr[rows, Z_U + g * MLP_GC:Z_U + (g + 1) * MLP_GC].astype(F32)
            g_b = z_scr[rows, Z_GB + g * MLP_GC:Z_GB + (g + 1) * MLP_GC].astype(F32)
            o_b = u * sg * _silu(g_b)
            o_ref[0, rows, GLA_WIDTH + g * MLP_GC:GLA_WIDTH + (g + 1) * MLP_GC] = o_b.astype(o_ref.dtype)

    @pl.when(t_idx == pl.num_programs(1) - 1)
    def _():
        for h in range(GLA_HEADS):
            sfin_ref[0, h] = st_scr[h].T


def _layer_in(h, s0, p, w_s, b_s, *, chunk, emit_vn):
    nb, seq, _ = h.shape
    t = min(256, seq)
    n_cols = 1024
    state_bytes = GLA_HEADS * GLA_DK * GLA_DV * 4
    pipelined = t * D_MODEL * 2 + t * D_MODEL * 2 + 2 * state_bytes
    resident = (2 * D_MODEL * Z_HALF * 2 + D_MODEL * LANES * 2 + LANES * GLA_KEY_WIDTH * 4
                + MLP_GROUPS * chunk * (chunk + LANES) * 4
                + t * Z_MAIN * 2 + t * GLA_KEY_WIDTH * 4 + state_bytes)
    out_specs = [
        pl.BlockSpec((1, t, D_MODEL), lambda b, i: (b, i, 0)),
        pl.BlockSpec((1, GLA_HEADS, GLA_DK, GLA_DV), lambda b, i: (b, 0, 0, 0)),
    ]
    out_shape = [
        jax.ShapeDtypeStruct((nb, seq, D_MODEL), BF16),
        jax.ShapeDtypeStruct((nb, GLA_HEADS, GLA_DK, GLA_DV), F32),
    ]
    if emit_vn:
        out_specs.append(pl.BlockSpec((1, t, MLP_WIDTH), lambda b, i: (b, i, 0)))
        out_shape.append(jax.ShapeDtypeStruct((nb, seq, MLP_WIDTH), F32))
        pipelined += t * MLP_WIDTH * 4
    const2 = lambda b, i: (0, 0)
    const3 = lambda b, i: (0, 0, 0)
    return pl.pallas_call(
        functools.partial(_layer_in_kernel, chunk=chunk, emit_vn=emit_vn, n_cols=n_cols),
        grid=(nb, seq // t),
        in_specs=[
            pl.BlockSpec((1, t, D_MODEL), lambda b, i: (b, i, 0)),
            _resident((D_MODEL, Z_HALF), const2),
            _resident((D_MODEL, Z_HALF), const2),
            _resident((D_MODEL, LANES), const2),
            _resident((LANES, GLA_KEY_WIDTH), const2),
            _resident((1, GLA_KEY_WIDTH), const2),
            pl.BlockSpec((1, GLA_HEADS, GLA_DK, GLA_DV), lambda b, i: (b, 0, 0, 0)),
            _resident((1, GLA_DV), const2),
            _resident((1, MLP_WIDTH), const2),
            _resident((1, MLP_WIDTH), const2),
            _resident((MLP_GROUPS, chunk, chunk), const3),
            _resident((MLP_GROUPS, chunk, 1), const3),
        ],
        out_specs=out_specs,
        out_shape=out_shape,
        scratch_shapes=[pltpu.VMEM((t, Z_MAIN), BF16),
                        pltpu.VMEM((t, GLA_KEY_WIDTH), F32),
                        pltpu.VMEM((GLA_HEADS, GLA_DV, GLA_DK), F32)],
        compiler_params=pltpu.CompilerParams(
            dimension_semantics=("arbitrary", "arbitrary"),
            vmem_limit_bytes=_vmem_limit(pipelined, resident)),
        name="layer_in",
    )(h, p["w_a"], p["w_b"], p["w_lr"], p["w_g"], p["b_g"], s0, p["gla_norm_g"], p["ln_g"], p["ln_b"], w_s, b_s)


def _layer_out_kernel(o_ref, x_ref, w_ref, g_ref, *out_refs, last):
    y = x_ref[...] + _dot(o_ref[...], w_ref[...])
    yn = _rmsnorm(y, g_ref[...])
    if last:
        out_refs[0][...] = yn
    else:
        out_refs[0][...] = y
        out_refs[1][...] = yn.astype(out_refs[1].dtype)


def _layer_out(o2d, x2d, w_out, g_next, *, last):
    rows = x2d.shape[0]
    tm = min(512, rows)
    row_spec = pl.BlockSpec((tm, D_MODEL), lambda i: (i, 0))
    out_specs = [row_spec]
    out_shape = [jax.ShapeDtypeStruct((rows, D_MODEL), F32)]
    pipelined = tm * D_MODEL * (2 + 4 + 4)
    if not last:
        out_specs.append(row_spec)
        out_shape.append(jax.ShapeDtypeStruct((rows, D_MODEL), BF16))
        pipelined += tm * D_MODEL * 2
    return pl.pallas_call(
        functools.partial(_layer_out_kernel, last=last),
        grid=(rows // tm,),
        in_specs=[row_spec, row_spec,
                  _resident((D_MODEL, D_MODEL), lambda i: (0, 0)),
                  _resident((1, D_MODEL), lambda i: (0, 0))],
        out_specs=out_specs,
        out_shape=out_shape,
        compiler_params=pltpu.CompilerParams(
            dimension_semantics=("arbitrary",),
            vmem_limit_bytes=_vmem_limit(pipelined, D_MODEL * D_MODEL * 2)),
        name="layer_out",
    )(o2d, x2d, w_out, g_next)


def _layer(x, h, s0, p, *, emit_vn, last):
    nb, seq, _ = x.shape
    chunk = min(seq, MLP_CHUNK)
    outs = _layer_in(h, s0, p, p["w_s"][:, :chunk, :chunk], p["b_s"][:, :chunk, None], chunk=chunk, emit_vn=emit_vn)
    o, s_fin = outs[0], outs[1]
    ys = _layer_out(o.reshape(nb * seq, D_MODEL), x.reshape(nb * seq, D_MODEL), p["w_out"], p["g_next"], last=last)
    y = ys[0].reshape(nb, seq, D_MODEL)
    h_next = None if last else ys[1].reshape(nb, seq, D_MODEL)
    return y, h_next, s_fin, (outs[2] if emit_vn else None)


def kernel(x_prompt, x_sample, state_gla, w_in, w_gate_up, b_gate, w_s, b_s, norm_g, gla_norm_g,
           mlp_ln_g, mlp_ln_b, w_out, final_norm_g):
    w_a = w_in[:, :, :Z_LR].astype(BF16)
    w_b = w_in[:, :, Z_LR + GATE_RANK:].astype(BF16)
    w_lr = jnp.pad(w_in[:, :, Z_LR:Z_LR + GATE_RANK], ((0, 0), (0, 0), (0, LANES - GATE_RANK))).astype(BF16)
    w_g = jnp.pad(w_gate_up, ((0, 0), (0, LANES - GATE_RANK), (0, 0)))
    w_out_b = w_out.astype(BF16)

    yp, ys = x_prompt, x_sample
    hp = _norm_in(yp.reshape(-1, D_MODEL), norm_g[0][None]).reshape(yp.shape)
    hs = _norm_in(ys.reshape(-1, D_MODEL), norm_g[0][None]).reshape(ys.shape)
    gla_p, gla_s, v_s = [], [], []
    for l in range(DEPTH):
        last = l == DEPTH - 1
        g_next = final_norm_g if last else norm_g[l + 1]
        p = dict(w_a=w_a[l], w_b=w_b[l], w_lr=w_lr[l], w_g=w_g[l], b_g=b_gate[l][None],
                 gla_norm_g=gla_norm_g[l][None], ln_g=mlp_ln_g[l][None], ln_b=mlp_ln_b[l][None],
                 w_s=w_s[l], b_s=b_s[l], w_out=w_out_b[l], g_next=g_next[None])
        s0_p = jnp.zeros((yp.shape[0], GLA_HEADS, GLA_DK, GLA_DV), F32)
        yp, hp, sp, _ = _layer(yp, hp, s0_p, p, emit_vn=False, last=last)
        ys, hs, ss, vs = _layer(ys, hs, state_gla[l], p, emit_vn=True, last=last)
        gla_p.append(sp)
        gla_s.append(ss)
        v_s.append(vs)
    return (yp, ys, jnp.stack(gla_p), jnp.stack(gla_s), jnp.stack(v_s))
```

```python
import functools

import jax
import jax.numpy as jnp
from jax import lax
from jax.experimental import pallas as pl
from jax.experimental.pallas import tpu as pltpu

D_MODEL = 2048
DEPTH = 4
GLA_HEADS = 4
GLA_DK = 128
GLA_DV = 256
GLA_KEY_WIDTH = GLA_HEADS * GLA_DK
GLA_WIDTH = GLA_HEADS * GLA_DV
GATE_RANK = 16
GATE_TEMP = 16.0
MLP_GROUPS = 4
MLP_GC = 256
MLP_WIDTH = MLP_GROUPS * MLP_GC
MLP_CHUNK = 128
EPS = 1e-6

Z_Q = 0
Z_K = Z_Q + GLA_KEY_WIDTH
Z_V = Z_K + GLA_KEY_WIDTH
Z_GA = Z_V + GLA_WIDTH
Z_LR = Z_GA + GLA_WIDTH
Z_U = Z_GA + GLA_WIDTH
Z_VM = Z_U + MLP_WIDTH
Z_GB = Z_VM + MLP_WIDTH
Z_MAIN = Z_GB + MLP_WIDTH
Z_HALF = Z_MAIN // 2

LANES = 128
V7X_VMEM_BYTES = 64 * 1024 * 1024
VMEM_INTERNAL_BYTES = 12 * 1024 * 1024

F32 = jnp.float32
BF16 = jnp.bfloat16


def _vmem_limit(pipelined_bytes, resident_bytes=0):
    need = 2 * pipelined_bytes + resident_bytes + VMEM_INTERNAL_BYTES
    return min(need, V7X_VMEM_BYTES - 6 * 1024 * 1024)


def _dot(a, b):
    return jnp.dot(a, b, preferred_element_type=F32)


def _dot_nt(a, b):
    return lax.dot_general(a, b, (((1,), (1,)), ((), ())), preferred_element_type=F32)


def _dot_tn(a, b):
    return lax.dot_general(a, b, (((0,), (0,)), ((), ())), preferred_element_type=F32)


def _split_bf16(x, parts):
    out = []
    r = x
    for _ in range(parts - 1):
        p = r.astype(BF16)
        out.append(p)
        r = r - p.astype(F32)
    out.append(r.astype(BF16))
    return out


def _silu(x):
    hx = 0.5 * x
    return hx + hx * jnp.tanh(hx)


def _rmsnorm(x, g):
    ms = jnp.mean(x * x, axis=-1, keepdims=True)
    return x * lax.rsqrt(ms + EPS) * g


def _resident(block_shape, index_map):
    return pl.BlockSpec(block_shape, index_map, pipeline_mode=pl.Buffered(1))


def _norm_in_kernel(x_ref, g_ref, h_ref):
    h_ref[...] = _rmsnorm(x_ref[...], g_ref[...]).astype(h_ref.dtype)


def _norm_in(x2d, g):
    rows = x2d.shape[0]
    tm = min(512, rows)
    return pl.pallas_call(
        _norm_in_kernel,
        grid=(rows // tm,),
        in_specs=[pl.BlockSpec((tm, D_MODEL), lambda i: (i, 0)),
                  pl.BlockSpec((1, D_MODEL), lambda i: (0, 0))],
        out_specs=pl.BlockSpec((tm, D_MODEL), lambda i: (i, 0)),
        out_shape=jax.ShapeDtypeStruct((rows, D_MODEL), BF16),
        compiler_params=pltpu.CompilerParams(
            dimension_semantics=("arbitrary",),
            vmem_limit_bytes=_vmem_limit(tm * D_MODEL * 6)),
        name="norm_in",
    )(x2d, g)


def _project_jobs(h_ref, wa_ref, wb_ref, wlr_ref, wg_ref, bg_ref, z_out, la_out, n_cols):
    def cols_job(w_ref, src, dst):
        z_out[:, dst:dst + n_cols] = _dot(h_ref[0], w_ref[:, src:src + n_cols]).astype(BF16)

    def gate_job():
        wg_hi, wg_lo = _split_bf16(wg_ref[...], 2)
        lr_hi, lr_lo = _split_bf16(_dot(h_ref[0], wlr_ref[...]), 2)
        xg = _dot(lr_hi, wg_hi) + _dot(lr_lo, wg_hi) + _dot(lr_hi, wg_lo) + bg_ref[...]
        log_sig = jnp.minimum(xg, 0.0) - jnp.log(1.0 + jnp.exp(-jnp.abs(xg)))
        la_out[...] = log_sig * (1.0 / GATE_TEMP)

    jobs = [gate_job]
    for w_ref, base in ((wa_ref, 0), (wb_ref, Z_HALF)):
        for n in range(Z_HALF // n_cols):
            jobs.append(functools.partial(cols_job, w_ref, n * n_cols, base + n * n_cols))
    return jobs


def _mix(z_in, la_in, s0_ref, gng_ref, lng_ref, lnb_ref, ws_ref, bs_ref, o_ref, sfin_ref, vn_ref, st_scr,
         *, chunk, stream_per_chunk, side_jobs=()):
    c = chunk
    n_levels = c.bit_length() - 1
    n_chunks = z_in.shape[0] // c
    side_jobs = list(side_jobs)
    n_side = max(len(side_jobs), 1)
    n_units = n_chunks * (GLA_HEADS + MLP_GROUPS)
    units_started = 0

    def run_side_jobs():
        nonlocal units_started
        units_started += 1
        while side_jobs and len(side_jobs) * n_units > (n_units - units_started) * n_side:
            side_jobs.pop(0)()

    rt = lax.broadcasted_iota(jnp.int32, (c, c), 0)
    cs = lax.broadcasted_iota(jnp.int32, (c, c), 1)
    differ = rt ^ cs
    top_bit = jnp.zeros((c, c), jnp.int32)
    for j in range(n_levels):
        top_bit = top_bit + jnp.where(differ >= (1 << j), 1, 0)
    level = jnp.where(rt >= cs, top_bit, -1)
    tri_ones = jnp.where(rt >= cs, 1.0, 0.0).astype(BF16)
    w_tril = [jnp.where(rt >= cs, ws_ref[g], 0.0).astype(BF16) for g in range(MLP_GROUPS)]
    row_id = lax.broadcasted_iota(jnp.int32, (c, GLA_DK), 0)
    row_bit = [((row_id >> j) & 1) == 1 for j in range(n_levels)]
    q_scale = GLA_DK ** -0.5

    for ci in range(n_chunks):
        rows = slice(ci * c, (ci + 1) * c)

        b_all = sum(_dot(tri_ones, part) for part in _split_bf16(la_in[rows, :], 2))
        for h in range(GLA_HEADS):
            run_side_jobs()
            vs = slice(h * GLA_DV, (h + 1) * GLA_DV)
            b = b_all[:, h * GLA_DK:(h + 1) * GLA_DK]
            q = z_in[rows, Z_Q + h * GLA_DK:Z_Q + (h + 1) * GLA_DK]
            k = z_in[rows, Z_K + h * GLA_DK:Z_K + (h + 1) * GLA_DK]
            v = z_in[rows, Z_V + h * GLA_DV:Z_V + (h + 1) * GLA_DV]
            scores = jnp.where(level == 0, _dot_nt(q, k), 0.0)
            f = b
            for j in range(n_levels):
                half = 1 << j
                f_prev = pltpu.roll(f, half, 0)
                decay = jnp.exp(jnp.where(row_bit[j], b - f_prev, f - b)).astype(BF16)
                p = _dot_nt(q * decay, k * decay)
                scores = jnp.where(level == j + 1, p, scores)
                f = jnp.where(row_bit[j], f, pltpu.roll(f, c - half, 0))
            b_last = f
            st = s0_ref[ci, h].T if stream_per_chunk else st_scr[h]
            o = _dot(scores.astype(BF16), v) + _dot_nt(q * jnp.exp(b).astype(BF16), st.astype(BF16))
            k_dec = k * jnp.exp(b_last - b).astype(BF16)
            st_new = jnp.exp(b_last[0:1, :]) * st + _dot_tn(v, k_dec)
            if stream_per_chunk:
                sfin_ref[ci, h] = st_new.T
            else:
                st_scr[h] = st_new
            ms = jnp.mean(o * o, axis=-1, keepdims=True)
            inv = q_scale * lax.rsqrt(q_scale * q_scale * ms + EPS)
            g_a = z_in[rows, Z_GA + h * GLA_DV:Z_GA + (h + 1) * GLA_DV].astype(F32)
            o_a = o * inv * gng_ref[...] * _silu(g_a)
            o_ref[0, rows, vs] = o_a.astype(o_ref.dtype)

        for g in range(MLP_GROUPS):
            run_side_jobs()
            gs = slice(g * MLP_GC, (g + 1) * MLP_GC)
            vm = z_in[rows, Z_VM + g * MLP_GC:Z_VM + (g + 1) * MLP_GC].astype(F32)
            mu = jnp.mean(vm, axis=-1, keepdims=True)
            xc = vm - mu
            var = jnp.mean(xc * xc, axis=-1, keepdims=True)
            vn = xc * lax.rsqrt(var + EPS) * lng_ref[:, gs] + lnb_ref[:, gs]
            if vn_ref is not None:
                vn_ref[0, rows, gs] = vn
            sg = _dot(w_tril[g], vn.astype(BF16)) + bs_ref[g]
            u = z_in[rows, Z_U + g * MLP_GC:Z_U + (g + 1) * MLP_GC].astype(F32)
            g_b = z_in[rows, Z_GB + g * MLP_GC:Z_GB + (g + 1) * MLP_GC].astype(F32)
            o_b = u * sg * _silu(g_b)
            o_ref[0, rows, GLA_WIDTH + g * MLP_GC:GLA_WIDTH + (g + 1) * MLP_GC] = o_b.astype(o_ref.dtype)


def _layer_in_kernel(*refs, chunk, emit_vn, n_cols, pipelined):
    refs = list(refs)
    h_refs = [refs.pop(0) for _ in range(2 if pipelined else 1)]
    wa_ref, wb_ref, wlr_ref, wg_ref, bg_ref, s0_ref, gng_ref, lng_ref, lnb_ref, ws_ref, bs_ref = refs[:11]
    outs = refs[11:]
    o_ref, sfin_ref = outs[0], outs[1]
    vn_ref = outs[2] if emit_vn else None
    scratch = outs[3 if emit_vn else 2:]
    proj_w = (wa_ref, wb_ref, wlr_ref, wg_ref, bg_ref)
    mix_p = (s0_ref, gng_ref, lng_ref, lnb_ref, ws_ref, bs_ref, o_ref, sfin_ref, vn_ref)

    if not pipelined:
        z0, la0 = scratch
        for job in _project_jobs(h_refs[0], *proj_w, z0, la0, n_cols):
            job()
        _mix(z0, la0, *mix_p, None, chunk=chunk, stream_per_chunk=True)
        return

    z0, la0, z1, la1, st_scr = scratch
    h_first_ref, h_next_ref = h_refs
    b_idx, t_idx = pl.program_id(0), pl.program_id(1)
    step = b_idx * pl.num_programs(1) + t_idx

    @pl.when(step == 0)
    def _():
        for job in _project_jobs(h_first_ref, *proj_w, z0, la0, n_cols):
            job()

    @pl.when(t_idx == 0)
    def _():
        for h in range(GLA_HEADS):
            st_scr[h] = s0_ref[0, h].T

    def step_body(z_cur, la_cur, z_nxt, la_nxt):
        _mix(z_cur, la_cur, *mix_p, st_scr, chunk=chunk, stream_per_chunk=False,
             side_jobs=_project_jobs(h_next_ref, *proj_w, z_nxt, la_nxt, n_cols))

    @pl.when(step % 2 == 0)
    def _():
        step_body(z0, la0, z1, la1)

    @pl.when(step % 2 == 1)
    def _():
        step_body(z1, la1, z0, la0)

    @pl.when(t_idx == pl.num_programs(1) - 1)
    def _():
        for h in range(GLA_HEADS):
            sfin_ref[0, h] = st_scr[h].T


def _layer_in(h, s0, p, w_s, b_s, *, chunk, emit_vn, pipelined):
    nb, seq, _ = h.shape
    state_bytes = GLA_HEADS * GLA_DK * GLA_DV * 4
    n_cols = 256
    if pipelined:
        t = 256
        n_tiles = seq // t
        grid = (nb, n_tiles)
        s_blk = 1

        def next_tile(b, i):
            wrap = i + 1 == n_tiles
            return (jnp.where(wrap, jnp.minimum(b + 1, nb - 1), b), jnp.where(wrap, 0, i + 1), 0)

        h_specs = [_resident((1, t, D_MODEL), lambda b, i: (0, 0, 0)),
                   pl.BlockSpec((1, t, D_MODEL), next_tile)]
        h_args = [h, h]
        scratch_shapes = [pltpu.VMEM((t, Z_MAIN), BF16), pltpu.VMEM((t, GLA_KEY_WIDTH), F32),
                          pltpu.VMEM((t, Z_MAIN), BF16), pltpu.VMEM((t, GLA_KEY_WIDTH), F32),
                          pltpu.VMEM((GLA_HEADS, GLA_DV, GLA_DK), F32)]
        scratch_bytes = 2 * (t * Z_MAIN * 2 + t * GLA_KEY_WIDTH * 4) + state_bytes + t * D_MODEL * 2
    else:
        assert seq == chunk
        t = nb * seq
        h = h.reshape(1, t, D_MODEL)
        grid = (1, 1)
        s_blk = nb
        h_specs = [pl.BlockSpec((1, t, D_MODEL), lambda b, i: (0, 0, 0))]
        h_args = [h]
        scratch_shapes = [pltpu.VMEM((t, Z_MAIN), BF16), pltpu.VMEM((t, GLA_KEY_WIDTH), F32)]
        scratch_bytes = t * Z_MAIN * 2 + t * GLA_KEY_WIDTH * 4
    pipelined_bytes = t * D_MODEL * 2 + t * D_MODEL * 2 + 2 * s_blk * state_bytes
    resident = (2 * D_MODEL * Z_HALF * 2 + D_MODEL * LANES * 2 + LANES * GLA_KEY_WIDTH * 4
                + MLP_GROUPS * chunk * (chunk + LANES) * 4 + scratch_bytes)
    state_spec = pl.BlockSpec((s_blk, GLA_HEADS, GLA_DK, GLA_DV), lambda b, i: (b, 0, 0, 0))
    out_specs = [pl.BlockSpec((1, t, D_MODEL), lambda b, i: (b, i, 0)), state_spec]
    out_shape = [jax.ShapeDtypeStruct((grid[0], grid[1] * t, D_MODEL), BF16),
                 jax.ShapeDtypeStruct((nb, GLA_HEADS, GLA_DK, GLA_DV), F32)]
    if emit_vn:
        out_specs.append(pl.BlockSpec((1, t, MLP_WIDTH), lambda b, i: (b, i, 0)))
        out_shape.append(jax.ShapeDtypeStruct((grid[0], grid[1] * t, MLP_WIDTH), F32))
        pipelined_bytes += t * MLP_WIDTH * 4
    const2 = lambda b, i: (0, 0)
    const3 = lambda b, i: (0, 0, 0)
    outs = pl.pallas_call(
        functools.partial(_layer_in_kernel, chunk=chunk, emit_vn=emit_vn, n_cols=n_cols, pipelined=pipelined),
        grid=grid,
        in_specs=h_specs + [
            _resident((D_MODEL, Z_HALF), const2),
            _resident((D_MODEL, Z_HALF), const2),
            _resident((D_MODEL, LANES), const2),
            _resident((LANES, GLA_KEY_WIDTH), const2),
            _resident((1, GLA_KEY_WIDTH), const2),
            state_spec,
            _resident((1, GLA_DV), const2),
            _resident((1, MLP_WIDTH), const2),
            _resident((1, MLP_WIDTH), const2),
            _resident((MLP_GROUPS, chunk, chunk), const3),
            _resident((MLP_GROUPS, chunk, 1), const3),
        ],
        out_specs=out_specs,
        out_shape=out_shape,
        scratch_shapes=scratch_shapes,
        compiler_params=pltpu.CompilerParams(
            dimension_semantics=("arbitrary", "arbitrary"),
            vmem_limit_bytes=_vmem_limit(pipelined_bytes, resident)),
        name="layer_in",
    )(*h_args, p["w_a"], p["w_b"], p["w_lr"], p["w_g"], p["b_g"], s0, p["gla_norm_g"], p["ln_g"], p["ln_b"], w_s, b_s)
    return [outs[0].reshape(nb, seq, D_MODEL), outs[1]] + ([outs[2].reshape(nb, seq, MLP_WIDTH)] if emit_vn else [])


def _layer_out_kernel(o_ref, x_ref, w_ref, g_ref, *out_refs, last):
    y = x_ref[...] + _dot(o_ref[...], w_ref[...])
    yn = _rmsnorm(y, g_ref[...])
    if last:
        out_refs[0][...] = yn
    else:
        out_refs[0][...] = y
        out_refs[1][...] = yn.astype(out_refs[1].dtype)


def _layer_out(o2d, x2d, w_out, g_next, *, last):
    rows = x2d.shape[0]
    tm = min(512, rows)
    row_spec = pl.BlockSpec((tm, D_MODEL), lambda i: (i, 0))
    out_specs = [row_spec]
    out_shape = [jax.ShapeDtypeStruct((rows, D_MODEL), F32)]
    pipelined = tm * D_MODEL * (2 + 4 + 4)
    if not last:
        out_specs.append(row_spec)
        out_shape.append(jax.ShapeDtypeStruct((rows, D_MODEL), BF16))
        pipelined += tm * D_MODEL * 2
    return pl.pallas_call(
        functools.partial(_layer_out_kernel, last=last),
        grid=(rows // tm,),
        in_specs=[row_spec, row_spec,
                  _resident((D_MODEL, D_MODEL), lambda i: (0, 0)),
                  _resident((1, D_MODEL), lambda i: (0, 0))],
        out_specs=out_specs,
        out_shape=out_shape,
        compiler_params=pltpu.CompilerParams(
            dimension_semantics=("arbitrary",),
            vmem_limit_bytes=_vmem_limit(pipelined, D_MODEL * D_MODEL * 2)),
        name="layer_out",
    )(o2d, x2d, w_out, g_next)


def _layer(x, h, s0, p, *, emit_vn, last):
    nb, seq, _ = x.shape
    chunk = min(seq, MLP_CHUNK)
    outs = _layer_in(h, s0, p, p["w_s"][:, :chunk, :chunk], p["b_s"][:, :chunk, None], chunk=chunk, emit_vn=emit_vn,
                     pipelined=seq > chunk)
    o, s_fin = outs[0], outs[1]
    ys = _layer_out(o.reshape(nb * seq, D_MODEL), x.reshape(nb * seq, D_MODEL), p["w_out"], p["g_next"], last=last)
    y = ys[0].reshape(nb, seq, D_MODEL)
    h_next = None if last else ys[1].reshape(nb, seq, D_MODEL)
    return y, h_next, s_fin, (outs[2] if emit_vn else None)


def kernel(x_prompt, x_sample, state_gla, w_in, w_gate_up, b_gate, w_s, b_s, norm_g, gla_norm_g,
           mlp_ln_g, mlp_ln_b, w_out, final_norm_g):
    w_a = w_in[:, :, :Z_LR].astype(BF16)
    w_b = w_in[:, :, Z_LR + GATE_RANK:].astype(BF16)
    w_lr = jnp.pad(w_in[:, :, Z_LR:Z_LR + GATE_RANK], ((0, 0), (0, 0), (0, LANES - GATE_RANK))).astype(BF16)
    w_g = jnp.pad(w_gate_up, ((0, 0), (0, LANES - GATE_RANK), (0, 0)))
    w_out_b = w_out.astype(BF16)

    yp, ys = x_prompt, x_sample
    hp = _norm_in(yp.reshape(-1, D_MODEL), norm_g[0][None]).reshape(yp.shape)
    hs = _norm_in(ys.reshape(-1, D_MODEL), norm_g[0][None]).reshape(ys.shape)
    gla_p, gla_s, v_s = [], [], []
    for l in range(DEPTH):
        last = l == DEPTH - 1
        g_next = final_norm_g if last else norm_g[l + 1]
        p = dict(w_a=w_a[l], w_b=w_b[l], w_lr=w_lr[l], w_g=w_g[l], b_g=b_gate[l][None],
                 gla_norm_g=gla_norm_g[l][None], ln_g=mlp_ln_g[l][None], ln_b=mlp_ln_b[l][None],
                 w_s=w_s[l], b_s=b_s[l], w_out=w_out_b[l], g_next=g_next[None])
        s0_p = jnp.zeros((yp.shape[0], GLA_HEADS, GLA_DK, GLA_DV), F32)
        yp, hp, sp, _ = _layer(yp, hp, s0_p, p, emit_vn=False, last=last)
        ys, hs, ss, vs = _layer(ys, hs, state_gla[l], p, emit_vn=True, last=last)
        gla_p.append(sp)
        gla_s.append(ss)
        v_s.append(vs)
    return (yp, ys, jnp.stack(gla_p), jnp.stack(gla_s), jnp.stack(v_s))
```

```python
import functools

import jax
import jax.numpy as jnp
from jax import lax
from jax.experimental import pallas as pl
from jax.experimental.pallas import tpu as pltpu

D_MODEL = 2048
DEPTH = 4
GLA_HEADS = 4
GLA_DK = 128
GLA_DV = 256
GLA_KEY_WIDTH = GLA_HEADS * GLA_DK
GLA_WIDTH = GLA_HEADS * GLA_DV
GATE_RANK = 16
GATE_TEMP = 16.0
MLP_GROUPS = 4
MLP_GC = 256
MLP_WIDTH = MLP_GROUPS * MLP_GC
MLP_CHUNK = 128
EPS = 1e-6

Z_Q = 0
Z_K = Z_Q + GLA_KEY_WIDTH
Z_V = Z_K + GLA_KEY_WIDTH
Z_GA = Z_V + GLA_WIDTH
Z_LR = Z_GA + GLA_WIDTH
Z_U = Z_GA + GLA_WIDTH
Z_VM = Z_U + MLP_WIDTH
Z_GB = Z_VM + MLP_WIDTH
Z_MAIN = Z_GB + MLP_WIDTH
Z_HALF = Z_MAIN // 2
D_IN = Z_MAIN + GATE_RANK

LANES = 128
V7X_VMEM_BYTES = 64 * 1024 * 1024
VMEM_INTERNAL_BYTES = 12 * 1024 * 1024

F32 = jnp.float32
BF16 = jnp.bfloat16


def _vmem_limit(pipelined_bytes, resident_bytes=0):
    need = 2 * pipelined_bytes + resident_bytes + VMEM_INTERNAL_BYTES
    return min(need, V7X_VMEM_BYTES - 6 * 1024 * 1024)


def _dot(a, b):
    return jnp.dot(a, b, preferred_element_type=F32)


def _dot_nt(a, b):
    return lax.dot_general(a, b, (((1,), (1,)), ((), ())), preferred_element_type=F32)


def _dot_tn(a, b):
    return lax.dot_general(a, b, (((0,), (0,)), ((), ())), preferred_element_type=F32)


def _split_bf16(x, parts):
    out = []
    r = x
    for _ in range(parts - 1):
        p = r.astype(BF16)
        out.append(p)
        r = r - p.astype(F32)
    out.append(r.astype(BF16))
    return out


def _silu(x):
    hx = 0.5 * x
    return hx + hx * jnp.tanh(hx)


def _rmsnorm(x, g):
    ms = jnp.mean(x * x, axis=-1, keepdims=True)
    return x * lax.rsqrt(ms + EPS) * g


def _resident(block_shape, index_map):
    return pl.BlockSpec(block_shape, index_map, pipeline_mode=pl.Buffered(1))


def _layer_block(shape, layer):
    zeros = (0,) * len(shape)
    return _resident((None,) + tuple(shape), lambda *_: (layer,) + zeros)


PREP_COLS = 512
PREP_BLOCKS = Z_MAIN // PREP_COLS


def _prep_w_in_kernel(w_ref, tail_ref, wa_ref, wb_ref, wlr_ref, prev_scr):
    j = pl.program_id(1)
    n_a = Z_HALF // PREP_COLS
    shift = PREP_COLS - GATE_RANK

    @pl.when(j < n_a)
    def _():
        wa_ref[0] = w_ref[0].astype(BF16)

    @pl.when(j == n_a)
    def _():
        lane = lax.broadcasted_iota(jnp.int32, (D_MODEL, LANES), 1)
        wlr_ref[0] = jnp.where(lane < GATE_RANK, w_ref[0, :, :LANES], 0.0).astype(BF16)

    def shifted(first_lanes):
        body = pltpu.roll(prev_scr[...], shift, 1)
        lane = lax.broadcasted_iota(jnp.int32, (D_MODEL, LANES), 1)
        last = jnp.where(lane < LANES - GATE_RANK, body[:, PREP_COLS - LANES:],
                         pltpu.roll(first_lanes, LANES - GATE_RANK, 1))
        return jnp.concatenate([body[:, :PREP_COLS - LANES], last], axis=1).astype(BF16)

    @pl.when(jnp.logical_and(j > n_a, j < PREP_BLOCKS))
    def _():
        wb_ref[0] = shifted(w_ref[0, :, :LANES])

    @pl.when(j == PREP_BLOCKS)
    def _():
        wb_ref[0] = shifted(tail_ref[0])

    @pl.when(j < PREP_BLOCKS)
    def _():
        prev_scr[...] = w_ref[0]


def _prep_w_in(w_in):
    n_layers = w_in.shape[0]
    n_a = Z_HALF // PREP_COLS
    tail = jnp.pad(w_in[:, :, Z_MAIN:], ((0, 0), (0, 0), (0, LANES - GATE_RANK)))
    blk = D_MODEL * PREP_COLS
    return pl.pallas_call(
        _prep_w_in_kernel,
        grid=(n_layers, PREP_BLOCKS + 1),
        in_specs=[
            pl.BlockSpec((1, D_MODEL, PREP_COLS), lambda l, j: (l, 0, jnp.minimum(j, PREP_BLOCKS - 1))),
            pl.BlockSpec((1, D_MODEL, LANES), lambda l, j: (l, 0, 0)),
        ],
        out_specs=[
            pl.BlockSpec((1, D_MODEL, PREP_COLS), lambda l, j: (l, 0, jnp.minimum(j, n_a - 1))),
            pl.BlockSpec((1, D_MODEL, PREP_COLS), lambda l, j: (l, 0, jnp.clip(j - n_a - 1, 0, n_a - 1))),
            pl.BlockSpec((1, D_MODEL, LANES), lambda l, j: (l, 0, 0)),
        ],
        out_shape=[
            jax.ShapeDtypeStruct((n_layers, D_MODEL, Z_HALF), BF16),
            jax.ShapeDtypeStruct((n_layers, D_MODEL, Z_HALF), BF16),
            jax.ShapeDtypeStruct((n_layers, D_MODEL, LANES), BF16),
        ],
        scratch_shapes=[pltpu.VMEM((D_MODEL, PREP_COLS), F32)],
        compiler_params=pltpu.CompilerParams(
            dimension_semantics=("arbitrary", "arbitrary"),
            vmem_limit_bytes=_vmem_limit(blk * 4 + 2 * blk * 2 + 2 * D_MODEL * LANES * 4, blk * 4)),
        name="prep_w_in",
    )(w_in, tail)


def _project_jobs(load_h, wa_ref, wb_ref, wlr_ref, wg_ref, bg_ref, z_out, la_out, n_cols):
    def cols_job(w_ref, src, dst):
        z_out[:, dst:dst + n_cols] = _dot(load_h(), w_ref[:, src:src + n_cols]).astype(BF16)

    def gate_job():
        wg_hi, wg_lo = _split_bf16(wg_ref[...], 2)
        lr_hi, lr_lo = _split_bf16(_dot(load_h(), wlr_ref[...]), 2)
        xg = _dot(lr_hi, wg_hi) + _dot(lr_lo, wg_hi) + _dot(lr_hi, wg_lo) + bg_ref[...]
        log_sig = jnp.minimum(xg, 0.0) - jnp.log(1.0 + jnp.exp(-jnp.abs(xg)))
        la_out[...] = log_sig * (1.0 / GATE_TEMP)

    jobs = [gate_job]
    for w_ref, base in ((wa_ref, 0), (wb_ref, Z_HALF)):
        for n in range(Z_HALF // n_cols):
            jobs.append(functools.partial(cols_job, w_ref, n * n_cols, base + n * n_cols))
    return jobs


def _mix(z_in, la_in, s0_ref, gng_ref, lng_ref, lnb_ref, ws_ref, bs_ref, o_ref, sfin_ref, vn_ref, st_scr,
         *, chunk, stream_per_chunk, side_jobs=()):
    c = chunk
    n_levels = c.bit_length() - 1
    n_chunks = z_in.shape[0] // c
    side_jobs = list(side_jobs)
    n_side = max(len(side_jobs), 1)
    n_units = n_chunks * (GLA_HEADS + MLP_GROUPS)
    units_started = 0

    def run_side_jobs():
        nonlocal units_started
        units_started += 1
        while side_jobs and len(side_jobs) * n_units > (n_units - units_started) * n_side:
            side_jobs.pop(0)()

    rt = lax.broadcasted_iota(jnp.int32, (c, c), 0)
    cs = lax.broadcasted_iota(jnp.int32, (c, c), 1)
    differ = rt ^ cs
    top_bit = jnp.zeros((c, c), jnp.int32)
    for j in range(n_levels):
        top_bit = top_bit + jnp.where(differ >= (1 << j), 1, 0)
    level = jnp.where(rt >= cs, top_bit, -1)
    tri_ones = jnp.where(rt >= cs, 1.0, 0.0).astype(BF16)
    w_tril = [jnp.where(rt >= cs, ws_ref[g, :c, :c], 0.0).astype(BF16) for g in range(MLP_GROUPS)]
    row_id = lax.broadcasted_iota(jnp.int32, (c, GLA_DK), 0)
    row_bit = [((row_id >> j) & 1) == 1 for j in range(n_levels)]
    q_scale = GLA_DK ** -0.5

    for ci in range(n_chunks):
        rows = slice(ci * c, (ci + 1) * c)

        b_all = sum(_dot(tri_ones, part) for part in _split_bf16(la_in[rows, :], 2))
        for h in range(GLA_HEADS):
            run_side_jobs()
            vs = slice(h * GLA_DV, (h + 1) * GLA_DV)
            b = b_all[:, h * GLA_DK:(h + 1) * GLA_DK]
            q = z_in[rows, Z_Q + h * GLA_DK:Z_Q + (h + 1) * GLA_DK]
            k = z_in[rows, Z_K + h * GLA_DK:Z_K + (h + 1) * GLA_DK]
            v = z_in[rows, Z_V + h * GLA_DV:Z_V + (h + 1) * GLA_DV]
            scores = jnp.where(level == 0, _dot_nt(q, k), 0.0)
            f = b
            for j in range(n_levels):
                half = 1 << j
                f_prev = pltpu.roll(f, half, 0)
                decay = jnp.exp(jnp.where(row_bit[j], b - f_prev, f - b)).astype(BF16)
                p = _dot_nt(q * decay, k * decay)
                scores = jnp.where(level == j + 1, p, scores)
                f = jnp.where(row_bit[j], f, pltpu.roll(f, c - half, 0))
            b_last = f
            st = s0_ref[ci, h].T if stream_per_chunk else st_scr[h]
            o = _dot(scores.astype(BF16), v) + _dot_nt(q * jnp.exp(b).astype(BF16), st.astype(BF16))
            k_dec = k * jnp.exp(b_last - b).astype(BF16)
            st_new = jnp.exp(b_last[0:1, :]) * st + _dot_tn(v, k_dec)
            if stream_per_chunk:
                sfin_ref[ci, h] = st_new.T
            else:
                st_scr[h] = st_new
            ms = jnp.mean(o * o, axis=-1, keepdims=True)
            inv = q_scale * lax.rsqrt(q_scale * q_scale * ms + EPS)
            g_a = z_in[rows, Z_GA + h * GLA_DV:Z_GA + (h + 1) * GLA_DV].astype(F32)
            o_a = o * inv * gng_ref[...] * _silu(g_a)
            o_ref[0, rows, vs] = o_a.astype(o_ref.dtype)

        for g in range(MLP_GROUPS):
            run_side_jobs()
            gs = slice(g * MLP_GC, (g + 1) * MLP_GC)
            vm = z_in[rows, Z_VM + g * MLP_GC:Z_VM + (g + 1) * MLP_GC].astype(F32)
            mu = jnp.mean(vm, axis=-1, keepdims=True)
            xc = vm - mu
            var = jnp.mean(xc * xc, axis=-1, keepdims=True)
            vn = xc * lax.rsqrt(var + EPS) * lng_ref[:, gs] + lnb_ref[:, gs]
            if vn_ref is not None:
                vn_ref[0, rows, gs] = vn
            sg = _dot(w_tril[g], vn.astype(BF16)) + bs_ref[g, :c, :]
            u = z_in[rows, Z_U + g * MLP_GC:Z_U + (g + 1) * MLP_GC].astype(F32)
            g_b = z_in[rows, Z_GB + g * MLP_GC:Z_GB + (g + 1) * MLP_GC].astype(F32)
            o_b = u * sg * _silu(g_b)
            o_ref[0, rows, GLA_WIDTH + g * MLP_GC:GLA_WIDTH + (g + 1) * MLP_GC] = o_b.astype(o_ref.dtype)


def _layer_in_kernel(*refs, chunk, emit_vn, n_cols, n_tiles, norm_input):
    refs = list(refs)
    x_ref = refs.pop(0)
    ng_ref = refs.pop(0) if norm_input else None
    wa_ref, wb_ref, wlr_ref, wg_ref, bg_ref, s0_ref, gng_ref, lng_ref, lnb_ref, ws_ref, bs_ref = refs[:11]
    outs = refs[11:]
    o_ref, sfin_ref = outs[0], outs[1]
    vn_ref = outs[2] if emit_vn else None
    scratch = outs[3 if emit_vn else 2:]
    proj_w = (wa_ref, wb_ref, wlr_ref, wg_ref, bg_ref)
    mix_p = (s0_ref, gng_ref, lng_ref, lnb_ref, ws_ref, bs_ref, o_ref, sfin_ref, vn_ref)
    if norm_input:
        h_scr = scratch.pop()

        def load_h():
            return h_scr[...]
    else:
        def load_h():
            return x_ref[0]

    def normalise_input():
        if norm_input:
            h_scr[...] = _rmsnorm(x_ref[0], ng_ref[...]).astype(BF16)

    if n_tiles is None:
        z_scr, la_scr = scratch
        normalise_input()
        for job in _project_jobs(load_h, *proj_w, z_scr, la_scr, n_cols):
            job()
        _mix(z_scr, la_scr, *mix_p, None, chunk=chunk, stream_per_chunk=True)
        return

    z_scr, la_scr, st_scr = scratch
    step = pl.program_id(0)
    tile_in_stream = lax.rem(jnp.maximum(step - 1, 0), n_tiles)

    @pl.when(step == 0)
    def _():
        z_scr[1] = jnp.zeros(z_scr.shape[1:], z_scr.dtype)
        la_scr[1] = jnp.zeros(la_scr.shape[1:], la_scr.dtype)

    @pl.when(tile_in_stream == 0)
    def _():
        for h in range(GLA_HEADS):
            st_scr[h] = s0_ref[0, h].T

    for parity in range(2):
        @pl.when(lax.rem(step, 2) == parity)
        def _():
            nxt, cur = parity, 1 - parity
            normalise_input()
            _mix(z_scr.at[cur], la_scr.at[cur], *mix_p, st_scr, chunk=chunk, stream_per_chunk=False,
                 side_jobs=_project_jobs(load_h, *proj_w, z_scr.at[nxt], la_scr.at[nxt], n_cols))

    @pl.when(tile_in_stream == n_tiles - 1)
    def _():
        for h in range(GLA_HEADS):
            sfin_ref[0, h] = st_scr[h].T


def _layer_in(x, s0, s0_block, w, layer, *, chunk, emit_vn, pipelined, norm_input):
    nb, seq, _ = x.shape
    state_bytes = GLA_HEADS * GLA_DK * GLA_DV * 4
    n_cols = 256
    if pipelined:
        t = 256
        n_tiles = seq // t
        n_all = nb * n_tiles
        grid = (n_all + 1,)
        s_blk = 1
        mixed = lambda s: jnp.maximum(s - 1, 0)
        x_map = lambda s: (jnp.minimum(s, n_all - 1) // n_tiles, jnp.minimum(s, n_all - 1) % n_tiles, 0)
        row_map = lambda s: (mixed(s) // n_tiles, mixed(s) % n_tiles, 0)
        stream_of = lambda s: mixed(s) // n_tiles
        scratch_shapes = [pltpu.VMEM((2, t, Z_MAIN), BF16), pltpu.VMEM((2, t, GLA_KEY_WIDTH), F32),
                          pltpu.VMEM((GLA_HEADS, GLA_DV, GLA_DK), F32)]
        scratch_bytes = 2 * (t * Z_MAIN * 2 + t * GLA_KEY_WIDTH * 4) + state_bytes
        out_rows = (nb, seq)
    else:
        assert seq == chunk
        t = min(256, nb * seq)
        n_tiles = None
        s_blk = t // seq
        x = x.reshape(nb // s_blk, t, D_MODEL)
        grid = (nb // s_blk,)
        x_map = row_map = lambda s: (s, 0, 0)
        stream_of = lambda s: s
        scratch_shapes = [pltpu.VMEM((t, Z_MAIN), BF16), pltpu.VMEM((t, GLA_KEY_WIDTH), F32)]
        scratch_bytes = t * Z_MAIN * 2 + t * GLA_KEY_WIDTH * 4
        out_rows = (nb // s_blk, t)
    in_specs = [pl.BlockSpec((1, t, D_MODEL), x_map)]
    pipelined_bytes = t * D_MODEL * x.dtype.itemsize + t * D_MODEL * 2 + 2 * s_blk * state_bytes
    args = [x]
    if norm_input:
        in_specs.append(_layer_block((1, D_MODEL), layer))
        args.append(w["norm_g"])
        scratch_shapes.append(pltpu.VMEM((t, D_MODEL), BF16))
        scratch_bytes += t * D_MODEL * 2
    resident = (2 * D_MODEL * Z_HALF * 2 + D_MODEL * LANES * 2 + LANES * GLA_KEY_WIDTH * 4
                + MLP_GROUPS * MLP_CHUNK * (MLP_CHUNK + LANES) * 4 + scratch_bytes)
    state_out_spec = pl.BlockSpec((s_blk, GLA_HEADS, GLA_DK, GLA_DV), lambda s: (stream_of(s), 0, 0, 0))
    out_specs = [pl.BlockSpec((1, t, D_MODEL), row_map), state_out_spec]
    out_shape = [jax.ShapeDtypeStruct(out_rows + (D_MODEL,), BF16),
                 jax.ShapeDtypeStruct((nb, GLA_HEADS, GLA_DK, GLA_DV), F32)]
    if emit_vn:
        out_specs.append(pl.BlockSpec((1, t, MLP_WIDTH), row_map))
        out_shape.append(jax.ShapeDtypeStruct(out_rows + (MLP_WIDTH,), F32))
        pipelined_bytes += t * MLP_WIDTH * 4
    outs = pl.pallas_call(
        functools.partial(_layer_in_kernel, chunk=chunk, emit_vn=emit_vn, n_cols=n_cols, n_tiles=n_tiles,
                          norm_input=norm_input),
        grid=grid,
        in_specs=in_specs + [
            _layer_block((D_MODEL, Z_HALF), layer),
            _layer_block((D_MODEL, Z_HALF), layer),
            _layer_block((D_MODEL, LANES), layer),
            _layer_block((LANES, GLA_KEY_WIDTH), layer),
            _layer_block((1, GLA_KEY_WIDTH), layer),
            s0_block(s_blk, stream_of),
            _layer_block((1, GLA_DV), layer),
            _layer_block((1, MLP_WIDTH), layer),
            _layer_block((1, MLP_WIDTH), layer),
            _layer_block((MLP_GROUPS, MLP_CHUNK, MLP_CHUNK), layer),
            _layer_block((MLP_GROUPS, MLP_CHUNK, 1), layer),
        ],
        out_specs=out_specs,
        out_shape=out_shape,
        scratch_shapes=scratch_shapes,
        compiler_params=pltpu.CompilerParams(
            dimension_semantics=("arbitrary",),
            vmem_limit_bytes=_vmem_limit(pipelined_bytes, resident)),
        name="layer_in",
    )(*args, w["w_a"], w["w_b"], w["w_lr"], w["w_g"], w["b_g"], s0, w["gla_norm_g"], w["ln_g"], w["ln_b"],
      w["w_s"], w["b_s"])
    return [outs[0].reshape(nb, seq, D_MODEL), outs[1]] + ([outs[2].reshape(nb, seq, MLP_WIDTH)] if emit_vn else [])


def _layer_out_kernel(o_ref, x_ref, w_ref, g_ref, *out_refs, last):
    y = x_ref[...] + _dot(o_ref[...], w_ref[...])
    yn = _rmsnorm(y, g_ref[...])
    if last:
        out_refs[0][...] = yn
    else:
        out_refs[0][...] = y
        out_refs[1][...] = yn.astype(out_refs[1].dtype)


def _layer_out(o2d, x2d, w, layer, *, last):
    rows = x2d.shape[0]
    tm = min(512, rows)
    row_spec = pl.BlockSpec((tm, D_MODEL), lambda i: (i, 0))
    out_specs = [row_spec]
    out_shape = [jax.ShapeDtypeStruct((rows, D_MODEL), F32)]
    pipelined = tm * D_MODEL * (2 + 4 + 4)
    if not last:
        out_specs.append(row_spec)
        out_shape.append(jax.ShapeDtypeStruct((rows, D_MODEL), BF16))
        pipelined += tm * D_MODEL * 2
    return pl.pallas_call(
        functools.partial(_layer_out_kernel, last=last),
        grid=(rows // tm,),
        in_specs=[row_spec, row_spec,
                  _layer_block((D_MODEL, D_MODEL), layer),
                  _layer_block((1, D_MODEL), layer + 1)],
        out_specs=out_specs,
        out_shape=out_shape,
        compiler_params=pltpu.CompilerParams(
            dimension_semantics=("arbitrary",),
            vmem_limit_bytes=_vmem_limit(pipelined, D_MODEL * D_MODEL * 2)),
        name="layer_out",
    )(o2d, x2d, w["w_out"], w["norm_g"])


def _layer(x, h, s0, s0_block, w, layer, *, emit_vn):
    nb, seq, _ = x.shape
    chunk = min(seq, MLP_CHUNK)
    last = layer == DEPTH - 1
    outs = _layer_in(x if h is None else h, s0, s0_block, w, layer, chunk=chunk, emit_vn=emit_vn,
                     pipelined=seq > chunk, norm_input=h is None)
    o, s_fin = outs[0], outs[1]
    ys = _layer_out(o.reshape(nb * seq, D_MODEL), x.reshape(nb * seq, D_MODEL), w, layer, last=last)
    y = ys[0].reshape(nb, seq, D_MODEL)
    h_next = None if last else ys[1].reshape(nb, seq, D_MODEL)
    return y, h_next, s_fin, (outs[2] if emit_vn else None)


def kernel(x_prompt, x_sample, state_gla, w_in, w_gate_up, b_gate, w_s, b_s, norm_g, gla_norm_g,
           mlp_ln_g, mlp_ln_b, w_out, final_norm_g):
    w_a, w_b, w_lr = _prep_w_in(w_in)
    w = dict(
        w_a=w_a, w_b=w_b, w_lr=w_lr,
        w_g=jnp.pad(w_gate_up, ((0, 0), (0, LANES - GATE_RANK), (0, 0))),
        b_g=b_gate[:, None, :],
        gla_norm_g=gla_norm_g[:, None, :], ln_g=mlp_ln_g[:, None, :], ln_b=mlp_ln_b[:, None, :],
        w_s=w_s, b_s=b_s[..., None],
        w_out=w_out.astype(BF16),
        norm_g=jnp.concatenate([norm_g, final_norm_g[None]], axis=0)[:, None, :],
    )
    zero_state = jnp.zeros((x_prompt.shape[0], GLA_HEADS, GLA_DK, GLA_DV), F32)

    def prompt_state_block(n_streams, stream_of):
        return pl.BlockSpec((n_streams, GLA_HEADS, GLA_DK, GLA_DV), lambda s: (stream_of(s), 0, 0, 0))

    yp, ys = x_prompt, x_sample
    hp = hs = None
    gla_p, gla_s, v_s = [], [], []
    for l in range(DEPTH):
        def sample_state_block(n_streams, stream_of, l=l):
            return pl.BlockSpec((None, n_streams, GLA_HEADS, GLA_DK, GLA_DV),
                                lambda s: (l, stream_of(s), 0, 0, 0))

        yp, hp, sp, _ = _layer(yp, hp, zero_state, prompt_state_block, w, l, emit_vn=False)
        ys, hs, ss, vs = _layer(ys, hs, state_gla, sample_state_block, w, l, emit_vn=True)
        gla_p.append(sp)
        gla_s.append(ss)
        v_s.append(vs)
    return (yp, ys, jnp.stack(gla_p), jnp.stack(gla_s), jnp.stack(v_s))
```

```python
import functools

import jax
import jax.numpy as jnp
from jax import lax
from jax.experimental import pallas as pl
from jax.experimental.pallas import tpu as pltpu

D_MODEL = 2048
DEPTH = 4
GLA_HEADS = 4
GLA_DK = 128
GLA_DV = 256
GLA_KEY_WIDTH = GLA_HEADS * GLA_DK
GLA_WIDTH = GLA_HEADS * GLA_DV
GATE_RANK = 16
GATE_TEMP = 16.0
MLP_GROUPS = 4
MLP_GC = 256
MLP_WIDTH = MLP_GROUPS * MLP_GC
MLP_CHUNK = 128
EPS = 1e-6

Z_Q = 0
Z_K = Z_Q + GLA_KEY_WIDTH
Z_V = Z_K + GLA_KEY_WIDTH
Z_GA = Z_V + GLA_WIDTH
Z_LR = Z_GA + GLA_WIDTH
Z_U = Z_GA + GLA_WIDTH
Z_VM = Z_U + MLP_WIDTH
Z_GB = Z_VM + MLP_WIDTH
Z_MAIN = Z_GB + MLP_WIDTH
D_IN = Z_MAIN + GATE_RANK

LANES = 128
V7X_VMEM_BYTES = 64 * 1024 * 1024
VMEM_INTERNAL_BYTES = 12 * 1024 * 1024

F32 = jnp.float32
BF16 = jnp.bfloat16


def _vmem_limit(pipelined_bytes, resident_bytes=0):
    need = 2 * pipelined_bytes + resident_bytes + VMEM_INTERNAL_BYTES
    return min(need, V7X_VMEM_BYTES - 6 * 1024 * 1024)


def _dot(a, b):
    return jnp.dot(a, b, preferred_element_type=F32)


def _dot_nt(a, b):
    return lax.dot_general(a, b, (((1,), (1,)), ((), ())), preferred_element_type=F32)


def _dot_tn(a, b):
    return lax.dot_general(a, b, (((0,), (0,)), ((), ())), preferred_element_type=F32)


def _split_bf16(x, parts):
    out = []
    r = x
    for _ in range(parts - 1):
        p = r.astype(BF16)
        out.append(p)
        r = r - p.astype(F32)
    out.append(r.astype(BF16))
    return out


def _silu(x):
    hx = 0.5 * x
    return hx + hx * jnp.tanh(hx)


def _rmsnorm(x, g):
    ms = jnp.mean(x * x, axis=-1, keepdims=True)
    return x * lax.rsqrt(ms + EPS) * g


def _resident(block_shape, index_map):
    return pl.BlockSpec(block_shape, index_map, pipeline_mode=pl.Buffered(1))


def _layer_block(shape, layer):
    zeros = (0,) * len(shape)
    return _resident((None,) + tuple(shape), lambda *_: (layer,) + zeros)


PREP_ROWS = 512


def _prep_w_in_kernel(w_ref, o_ref):
    o_ref[...] = w_ref[0].T.astype(o_ref.dtype)


def _prep_w_in(w_in_t):
    n_layers = w_in_t.shape[0]
    blk = PREP_ROWS * D_MODEL

    def src_rows(l, m):
        past_gate = (m * PREP_ROWS >= Z_LR).astype(jnp.int32)
        return (l, (m * (PREP_ROWS // GATE_RANK) + past_gate) * GATE_RANK, 0)

    return pl.pallas_call(
        _prep_w_in_kernel,
        grid=(n_layers, Z_MAIN // PREP_ROWS),
        in_specs=[pl.BlockSpec((pl.Element(1), pl.Element(PREP_ROWS), pl.Element(D_MODEL)), src_rows)],
        out_specs=pl.BlockSpec((None, D_MODEL, PREP_ROWS), lambda l, m: (l, 0, m)),
        out_shape=jax.ShapeDtypeStruct((n_layers, D_MODEL, Z_MAIN), BF16),
        compiler_params=pltpu.CompilerParams(
            dimension_semantics=("arbitrary", "arbitrary"),
            vmem_limit_bytes=_vmem_limit(blk * 4 + blk * 2)),
        name="prep_w_in",
    )(w_in_t)


def _project_jobs(load_h, w_ref, wlr_ref, wg_ref, bg_ref, z_out, la_out, n_cols):
    def cols_job(col):
        z_out[:, col:col + n_cols] = _dot(load_h(), w_ref[:, col:col + n_cols]).astype(BF16)

    def gate_job():
        wg_hi, wg_lo = _split_bf16(wg_ref[...], 2)
        lr_hi, lr_lo = _split_bf16(_dot(load_h(), wlr_ref[...]), 2)
        xg = _dot(lr_hi, wg_hi) + _dot(lr_lo, wg_hi) + _dot(lr_hi, wg_lo) + bg_ref[...]
        log_sig = jnp.minimum(xg, 0.0) - jnp.log(1.0 + jnp.exp(-jnp.abs(xg)))
        la_out[...] = log_sig * (1.0 / GATE_TEMP)

    return [gate_job] + [functools.partial(cols_job, col) for col in range(0, Z_MAIN, n_cols)]


def _mix(z_in, la_in, s0_ref, gng_ref, lng_ref, lnb_ref, ws_ref, bs_ref, o_ref, sfin_ref, vn_ref, st_scr,
         *, chunk, stream_per_chunk, side_jobs=()):
    c = chunk
    n_levels = c.bit_length() - 1
    n_chunks = z_in.shape[0] // c
    side_jobs = list(side_jobs)
    n_side = max(len(side_jobs), 1)
    n_units = n_chunks * (GLA_HEADS + MLP_GROUPS)
    units_started = 0

    def run_side_jobs():
        nonlocal units_started
        units_started += 1
        while side_jobs and len(side_jobs) * n_units > (n_units - units_started) * n_side:
            side_jobs.pop(0)()

    rt = lax.broadcasted_iota(jnp.int32, (c, c), 0)
    cs = lax.broadcasted_iota(jnp.int32, (c, c), 1)
    differ = rt ^ cs
    top_bit = jnp.zeros((c, c), jnp.int32)
    for j in range(n_levels):
        top_bit = top_bit + jnp.where(differ >= (1 << j), 1, 0)
    level = jnp.where(rt >= cs, top_bit, -1)
    tri_ones = jnp.where(rt >= cs, 1.0, 0.0).astype(BF16)
    w_tril = [jnp.where(rt >= cs, ws_ref[g, :c, :c], 0.0).astype(BF16) for g in range(MLP_GROUPS)]
    row_id = lax.broadcasted_iota(jnp.int32, (c, GLA_DK), 0)
    row_bit = [((row_id >> j) & 1) == 1 for j in range(n_levels)]
    q_scale = GLA_DK ** -0.5

    for ci in range(n_chunks):
        rows = slice(ci * c, (ci + 1) * c)

        b_all = sum(_dot(tri_ones, part) for part in _split_bf16(la_in[rows, :], 2))
        for h in range(GLA_HEADS):
            run_side_jobs()
            vs = slice(h * GLA_DV, (h + 1) * GLA_DV)
            b = b_all[:, h * GLA_DK:(h + 1) * GLA_DK]
            q = z_in[rows, Z_Q + h * GLA_DK:Z_Q + (h + 1) * GLA_DK]
            k = z_in[rows, Z_K + h * GLA_DK:Z_K + (h + 1) * GLA_DK]
            v = z_in[rows, Z_V + h * GLA_DV:Z_V + (h + 1) * GLA_DV]
            scores = jnp.where(level == 0, _dot_nt(q, k), 0.0)
            f = b
            for j in range(n_levels):
                half = 1 << j
                f_prev = pltpu.roll(f, half, 0)
                decay = jnp.exp(jnp.where(row_bit[j], b - f_prev, f - b)).astype(BF16)
                p = _dot_nt(q * decay, k * decay)
                scores = jnp.where(level == j + 1, p, scores)
                f = jnp.where(row_bit[j], f, pltpu.roll(f, c - half, 0))
            b_last = f
            st = s0_ref[ci, h].T if stream_per_chunk else st_scr[h]
            o = _dot(scores.astype(BF16), v) + _dot_nt(q * jnp.exp(b).astype(BF16), st.astype(BF16))
            k_dec = k * jnp.exp(b_last - b).astype(BF16)
            st_new = jnp.exp(b_last[0:1, :]) * st + _dot_tn(v, k_dec)
            if stream_per_chunk:
                sfin_ref[ci, h] = st_new.T
            else:
                st_scr[h] = st_new
            ms = jnp.mean(o * o, axis=-1, keepdims=True)
            inv = q_scale * lax.rsqrt(q_scale * q_scale * ms + EPS)
            g_a = z_in[rows, Z_GA + h * GLA_DV:Z_GA + (h + 1) * GLA_DV].astype(F32)
            o_a = o * inv * gng_ref[...] * _silu(g_a)
            o_ref[0, rows, vs] = o_a.astype(o_ref.dtype)

        for g in range(MLP_GROUPS):
            run_side_jobs()
            gs = slice(g * MLP_GC, (g + 1) * MLP_GC)
            vm = z_in[rows, Z_VM + g * MLP_GC:Z_VM + (g + 1) * MLP_GC].astype(F32)
            mu = jnp.mean(vm, axis=-1, keepdims=True)
            xc = vm - mu
            var = jnp.mean(xc * xc, axis=-1, keepdims=True)
            vn = xc * lax.rsqrt(var + EPS) * lng_ref[:, gs] + lnb_ref[:, gs]
            if vn_ref is not None:
                vn_ref[0, rows, gs] = vn
            sg = _dot(w_tril[g], vn.astype(BF16)) + bs_ref[g, :c, :]
            u = z_in[rows, Z_U + g * MLP_GC:Z_U + (g + 1) * MLP_GC].astype(F32)
            g_b = z_in[rows, Z_GB + g * MLP_GC:Z_GB + (g + 1) * MLP_GC].astype(F32)
            o_b = u * sg * _silu(g_b)
            o_ref[0, rows, GLA_WIDTH + g * MLP_GC:GLA_WIDTH + (g + 1) * MLP_GC] = o_b.astype(o_ref.dtype)


def _layer_in_kernel(*refs, chunk, emit_vn, n_cols, n_tiles, norm_input):
    refs = list(refs)
    x_ref = refs.pop(0)
    ng_ref = refs.pop(0) if norm_input else None
    w_ref, wlr_ref, wg_ref, bg_ref, s0_ref, gng_ref, lng_ref, lnb_ref, ws_ref, bs_ref = refs[:10]
    outs = refs[10:]
    o_ref, sfin_ref = outs[0], outs[1]
    vn_ref = outs[2] if emit_vn else None
    scratch = outs[3 if emit_vn else 2:]
    proj_w = (w_ref, wlr_ref, wg_ref, bg_ref)
    mix_p = (s0_ref, gng_ref, lng_ref, lnb_ref, ws_ref, bs_ref, o_ref, sfin_ref, vn_ref)
    if norm_input:
        h_scr = scratch.pop()

        def load_h():
            return h_scr[...]
    else:
        def load_h():
            return x_ref[0]

    def normalise_input():
        if norm_input:
            h_scr[...] = _rmsnorm(x_ref[0], ng_ref[...]).astype(BF16)

    if n_tiles is None:
        z_scr, la_scr = scratch
        normalise_input()
        for job in _project_jobs(load_h, *proj_w, z_scr, la_scr, n_cols):
            job()
        _mix(z_scr, la_scr, *mix_p, None, chunk=chunk, stream_per_chunk=True)
        return

    z_scr, la_scr, st_scr = scratch
    step = pl.program_id(0)
    tile_in_stream = lax.rem(jnp.maximum(step - 1, 0), n_tiles)

    @pl.when(step == 0)
    def _():
        z_scr[1] = jnp.zeros(z_scr.shape[1:], z_scr.dtype)
        la_scr[1] = jnp.zeros(la_scr.shape[1:], la_scr.dtype)

    @pl.when(tile_in_stream == 0)
    def _():
        for h in range(GLA_HEADS):
            st_scr[h] = s0_ref[0, h].T

    for parity in range(2):
        @pl.when(lax.rem(step, 2) == parity)
        def _():
            nxt, cur = parity, 1 - parity
            normalise_input()
            _mix(z_scr.at[cur], la_scr.at[cur], *mix_p, st_scr, chunk=chunk, stream_per_chunk=False,
                 side_jobs=_project_jobs(load_h, *proj_w, z_scr.at[nxt], la_scr.at[nxt], n_cols))

    @pl.when(tile_in_stream == n_tiles - 1)
    def _():
        for h in range(GLA_HEADS):
            sfin_ref[0, h] = st_scr[h].T


def _layer_in(x, s0, s0_block, w, layer, *, chunk, emit_vn, pipelined, norm_input):
    nb, seq, _ = x.shape
    state_bytes = GLA_HEADS * GLA_DK * GLA_DV * 4
    n_cols = 256
    if pipelined:
        t = 256
        n_tiles = seq // t
        n_all = nb * n_tiles
        grid = (n_all + 1,)
        s_blk = 1
        mixed = lambda s: jnp.maximum(s - 1, 0)
        x_map = lambda s: (jnp.minimum(s, n_all - 1) // n_tiles, jnp.minimum(s, n_all - 1) % n_tiles, 0)
        row_map = lambda s: (mixed(s) // n_tiles, mixed(s) % n_tiles, 0)
        stream_of = lambda s: mixed(s) // n_tiles
        scratch_shapes = [pltpu.VMEM((2, t, Z_MAIN), BF16), pltpu.VMEM((2, t, GLA_KEY_WIDTH), F32),
                          pltpu.VMEM((GLA_HEADS, GLA_DV, GLA_DK), F32)]
        scratch_bytes = 2 * (t * Z_MAIN * 2 + t * GLA_KEY_WIDTH * 4) + state_bytes
        out_rows = (nb, seq)
    else:
        assert seq == chunk
        t = min(256, nb * seq)
        n_tiles = None
        s_blk = t // seq
        x = x.reshape(nb // s_blk, t, D_MODEL)
        grid = (nb // s_blk,)
        x_map = row_map = lambda s: (s, 0, 0)
        stream_of = lambda s: s
        scratch_shapes = [pltpu.VMEM((t, Z_MAIN), BF16), pltpu.VMEM((t, GLA_KEY_WIDTH), F32)]
        scratch_bytes = t * Z_MAIN * 2 + t * GLA_KEY_WIDTH * 4
        out_rows = (nb // s_blk, t)
    in_specs = [pl.BlockSpec((1, t, D_MODEL), x_map)]
    pipelined_bytes = t * D_MODEL * x.dtype.itemsize + t * D_MODEL * 2 + 2 * s_blk * state_bytes
    args = [x]
    if norm_input:
        in_specs.append(_layer_block((1, D_MODEL), layer))
        args.append(w["norm_g"])
        scratch_shapes.append(pltpu.VMEM((t, D_MODEL), BF16))
        scratch_bytes += t * D_MODEL * 2
    resident = (Z_MAIN * D_MODEL * 2 + LANES * D_MODEL * 2 + LANES * GLA_KEY_WIDTH * 4
                + MLP_GROUPS * MLP_CHUNK * (MLP_CHUNK + LANES) * 4 + scratch_bytes)
    state_out_spec = pl.BlockSpec((s_blk, GLA_HEADS, GLA_DK, GLA_DV), lambda s: (stream_of(s), 0, 0, 0))
    out_specs = [pl.BlockSpec((1, t, D_MODEL), row_map), state_out_spec]
    out_shape = [jax.ShapeDtypeStruct(out_rows + (D_MODEL,), BF16),
                 jax.ShapeDtypeStruct((nb, GLA_HEADS, GLA_DK, GLA_DV), F32)]
    if emit_vn:
        out_specs.append(pl.BlockSpec((1, t, MLP_WIDTH), row_map))
        out_shape.append(jax.ShapeDtypeStruct(out_rows + (MLP_WIDTH,), F32))
        pipelined_bytes += t * MLP_WIDTH * 4
    outs = pl.pallas_call(
        functools.partial(_layer_in_kernel, chunk=chunk, emit_vn=emit_vn, n_cols=n_cols, n_tiles=n_tiles,
                          norm_input=norm_input),
        grid=grid,
        in_specs=in_specs + [
            _layer_block((D_MODEL, Z_MAIN), layer),
            _layer_block((D_MODEL, LANES), layer),
            _layer_block((LANES, GLA_KEY_WIDTH), layer),
            _layer_block((1, GLA_KEY_WIDTH), layer),
            s0_block(s_blk, stream_of),
            _layer_block((1, GLA_DV), layer),
            _layer_block((1, MLP_WIDTH), layer),
            _layer_block((1, MLP_WIDTH), layer),
            _layer_block((MLP_GROUPS, MLP_CHUNK, MLP_CHUNK), layer),
            _layer_block((MLP_GROUPS, MLP_CHUNK, 1), layer),
        ],
        out_specs=out_specs,
        out_shape=out_shape,
        scratch_shapes=scratch_shapes,
        compiler_params=pltpu.CompilerParams(
            dimension_semantics=("arbitrary",),
            vmem_limit_bytes=_vmem_limit(pipelined_bytes, resident)),
        name="layer_in",
    )(*args, w["w_main"], w["w_lr"], w["w_g"], w["b_g"], s0, w["gla_norm_g"], w["ln_g"], w["ln_b"],
      w["w_s"], w["b_s"])
    return [outs[0].reshape(nb, seq, D_MODEL), outs[1]] + ([outs[2].reshape(nb, seq, MLP_WIDTH)] if emit_vn else [])


def _layer_out_kernel(o_ref, x_ref, w_ref, g_ref, *out_refs, last):
    y = x_ref[...] + _dot(o_ref[...], w_ref[...])
    yn = _rmsnorm(y, g_ref[...])
    if last:
        out_refs[0][...] = yn
    else:
        out_refs[0][...] = y
        out_refs[1][...] = yn.astype(out_refs[1].dtype)


def _layer_out(o2d, x2d, w, layer, *, last):
    rows = x2d.shape[0]
    tm = min(512, rows)
    row_spec = pl.BlockSpec((tm, D_MODEL), lambda i: (i, 0))
    out_specs = [row_spec]
    out_shape = [jax.ShapeDtypeStruct((rows, D_MODEL), F32)]
    pipelined = tm * D_MODEL * (2 + 4 + 4)
    if not last:
        out_specs.append(row_spec)
        out_shape.append(jax.ShapeDtypeStruct((rows, D_MODEL), BF16))
        pipelined += tm * D_MODEL * 2
    return pl.pallas_call(
        functools.partial(_layer_out_kernel, last=last),
        grid=(rows // tm,),
        in_specs=[row_spec, row_spec,
                  _layer_block((D_MODEL, D_MODEL), layer),
                  _layer_block((1, D_MODEL), layer + 1)],
        out_specs=out_specs,
        out_shape=out_shape,
        compiler_params=pltpu.CompilerParams(
            dimension_semantics=("arbitrary",),
            vmem_limit_bytes=_vmem_limit(pipelined, D_MODEL * D_MODEL * 2)),
        name="layer_out",
    )(o2d, x2d, w["w_out"], w["norm_g"])


def _layer(x, h, s0, s0_block, w, layer, *, emit_vn):
    nb, seq, _ = x.shape
    chunk = min(seq, MLP_CHUNK)
    last = layer == DEPTH - 1
    outs = _layer_in(x if h is None else h, s0, s0_block, w, layer, chunk=chunk, emit_vn=emit_vn,
                     pipelined=seq > chunk, norm_input=h is None)
    o, s_fin = outs[0], outs[1]
    ys = _layer_out(o.reshape(nb * seq, D_MODEL), x.reshape(nb * seq, D_MODEL), w, layer, last=last)
    y = ys[0].reshape(nb, seq, D_MODEL)
    h_next = None if last else ys[1].reshape(nb, seq, D_MODEL)
    return y, h_next, s_fin, (outs[2] if emit_vn else None)


def kernel(x_prompt, x_sample, state_gla, w_in, w_gate_up, b_gate, w_s, b_s, norm_g, gla_norm_g,
           mlp_ln_g, mlp_ln_b, w_out, final_norm_g):
    w_lr = jnp.pad(w_in[:, :, Z_LR:Z_LR + GATE_RANK], ((0, 0), (0, 0), (0, LANES - GATE_RANK)))
    w = dict(
        w_main=_prep_w_in(jnp.swapaxes(w_in, 1, 2)), w_lr=w_lr.astype(BF16),
        w_g=jnp.pad(w_gate_up, ((0, 0), (0, LANES - GATE_RANK), (0, 0))),
        b_g=b_gate[:, None, :],
        gla_norm_g=gla_norm_g[:, None, :], ln_g=mlp_ln_g[:, None, :], ln_b=mlp_ln_b[:, None, :],
        w_s=w_s, b_s=b_s[..., None],
        w_out=w_out.astype(BF16),
        norm_g=jnp.concatenate([norm_g, final_norm_g[None]], axis=0)[:, None, :],
    )
    zero_state = jnp.zeros((x_prompt.shape[0], GLA_HEADS, GLA_DK, GLA_DV), F32)

    def prompt_state_block(n_streams, stream_of):
        return pl.BlockSpec((n_streams, GLA_HEADS, GLA_DK, GLA_DV), lambda s: (stream_of(s), 0, 0, 0))

    yp, ys = x_prompt, x_sample
    hp = hs = None
    gla_p, gla_s, v_s = [], [], []
    for l in range(DEPTH):
        def sample_state_block(n_streams, stream_of, l=l):
            return pl.BlockSpec((None, n_streams, GLA_HEADS, GLA_DK, GLA_DV),
                                lambda s: (l, stream_of(s), 0, 0, 0))

        yp, hp, sp, _ = _layer(yp, hp, zero_state, prompt_state_block, w, l, emit_vn=False)
        ys, hs, ss, vs = _layer(ys, hs, state_gla, sample_state_block, w, l, emit_vn=True)
        gla_p.append(sp)
        gla_s.append(ss)
        v_s.append(vs)
    return (yp, ys, jnp.stack(gla_p), jnp.stack(gla_s), jnp.stack(v_s))
```

```python
import functools

import jax
import jax.numpy as jnp
from jax import lax
from jax.experimental import pallas as pl
from jax.experimental.pallas import tpu as pltpu

D_MODEL = 2048
DEPTH = 4
GLA_HEADS = 4
GLA_DK = 128
GLA_DV = 256
GLA_KEY_WIDTH = GLA_HEADS * GLA_DK
GLA_WIDTH = GLA_HEADS * GLA_DV
GATE_RANK = 16
GATE_TEMP = 16.0
MLP_GROUPS = 4
MLP_GC = 256
MLP_WIDTH = MLP_GROUPS * MLP_GC
MLP_CHUNK = 128
EPS = 1e-6

Z_Q = 0
Z_K = Z_Q + GLA_KEY_WIDTH
Z_V = Z_K + GLA_KEY_WIDTH
Z_GA = Z_V + GLA_WIDTH
Z_LR = Z_GA + GLA_WIDTH
Z_U = Z_GA + GLA_WIDTH
Z_VM = Z_U + MLP_WIDTH
Z_GB = Z_VM + MLP_WIDTH
Z_MAIN = Z_GB + MLP_WIDTH
D_IN = Z_MAIN + GATE_RANK

LANES = 128
V7X_VMEM_BYTES = 64 * 1024 * 1024
VMEM_INTERNAL_BYTES = 12 * 1024 * 1024

F32 = jnp.float32
BF16 = jnp.bfloat16


def _vmem_limit(pipelined_bytes, resident_bytes=0):
    need = 2 * pipelined_bytes + resident_bytes + VMEM_INTERNAL_BYTES
    return min(need, V7X_VMEM_BYTES - 6 * 1024 * 1024)


def _dot(a, b):
    return jnp.dot(a, b, preferred_element_type=F32)


def _dot_nt(a, b):
    return lax.dot_general(a, b, (((1,), (1,)), ((), ())), preferred_element_type=F32)


def _dot_tn(a, b):
    return lax.dot_general(a, b, (((0,), (0,)), ((), ())), preferred_element_type=F32)


def _split_bf16(x, parts):
    out = []
    r = x
    for _ in range(parts - 1):
        p = r.astype(BF16)
        out.append(p)
        r = r - p.astype(F32)
    out.append(r.astype(BF16))
    return out


def _silu(x):
    hx = 0.5 * x
    return hx + hx * jnp.tanh(hx)


def _rmsnorm(x, g):
    ms = jnp.mean(x * x, axis=-1, keepdims=True)
    return x * lax.rsqrt(ms + EPS) * g


def _resident(block_shape, index_map):
    return pl.BlockSpec(block_shape, index_map, pipeline_mode=pl.Buffered(1))


def _layer_block(shape, layer):
    zeros = (0,) * len(shape)
    return _resident((None,) + tuple(shape), lambda *_: (layer,) + zeros)


PREP_ROWS = 512


def _prep_w_in_kernel(w_ref, o_ref):
    o_ref[...] = w_ref[0].T.astype(o_ref.dtype)


def _prep_w_in(w_in_t):
    n_layers = w_in_t.shape[0]
    blk = PREP_ROWS * D_MODEL

    def src_rows(l, m):
        past_gate = (m * PREP_ROWS >= Z_LR).astype(jnp.int32)
        return (l, (m * (PREP_ROWS // GATE_RANK) + past_gate) * GATE_RANK, 0)

    return pl.pallas_call(
        _prep_w_in_kernel,
        grid=(n_layers, Z_MAIN // PREP_ROWS),
        in_specs=[pl.BlockSpec((pl.Element(1), pl.Element(PREP_ROWS), pl.Element(D_MODEL)), src_rows)],
        out_specs=pl.BlockSpec((None, D_MODEL, PREP_ROWS), lambda l, m: (l, 0, m)),
        out_shape=jax.ShapeDtypeStruct((n_layers, D_MODEL, Z_MAIN), BF16),
        compiler_params=pltpu.CompilerParams(
            dimension_semantics=("arbitrary", "arbitrary"),
            vmem_limit_bytes=_vmem_limit(blk * 4 + blk * 2)),
        name="prep_w_in",
    )(w_in_t)


def _project_jobs(hx, w_ref, wlr_ref, wg_ref, bg_ref, z_out, la_out, n_cols):
    def cols_job(col):
        z_out[:, col:col + n_cols] = _dot(hx, w_ref[:, col:col + n_cols]).astype(BF16)

    def gate_job():
        lr = _dot(hx, wlr_ref[...])
        lane = lax.broadcasted_iota(jnp.int32, lr.shape, 1)
        lr_lo = lr - lr.astype(BF16).astype(F32)
        lhs = jnp.where((lane >= GATE_RANK) & (lane < 2 * GATE_RANK), lr_lo, lr).astype(BF16)
        wg = wg_ref[...]
        row = lax.broadcasted_iota(jnp.int32, wg.shape, 0)
        wg_lo = wg - wg.astype(BF16).astype(F32)
        rhs = jnp.where(row >= 2 * GATE_RANK, wg_lo, wg).astype(BF16)
        xg = _dot(lhs, rhs) + bg_ref[...]
        log_sig = jnp.minimum(xg, 0.0) - jnp.log(1.0 + jnp.exp(-jnp.abs(xg)))
        la_out[...] = log_sig * (1.0 / GATE_TEMP)

    return [gate_job] + [functools.partial(cols_job, col) for col in range(0, Z_MAIN, n_cols)]


def _mix(z_in, la_in, s0_ref, gng_ref, lng_ref, lnb_ref, ws_ref, bs_ref, o_ref, sfin_ref, vn_ref, st_scr,
         *, chunk, stream_per_chunk, side_jobs=()):
    c = chunk
    n_levels = c.bit_length() - 1
    n_chunks = z_in.shape[0] // c
    side_jobs = list(side_jobs)
    n_side = max(len(side_jobs), 1)
    n_units = n_chunks * (GLA_HEADS + MLP_GROUPS)
    units_started = 0

    def run_side_jobs():
        nonlocal units_started
        units_started += 1
        while side_jobs and len(side_jobs) * n_units > (n_units - units_started) * n_side:
            side_jobs.pop(0)()

    rt = lax.broadcasted_iota(jnp.int32, (c, c), 0)
    cs = lax.broadcasted_iota(jnp.int32, (c, c), 1)
    differ = rt ^ cs
    top_bit = jnp.zeros((c, c), jnp.int32)
    for j in range(n_levels):
        top_bit = top_bit + jnp.where(differ >= (1 << j), 1, 0)
    level = jnp.where(rt >= cs, top_bit, -1)
    tri_ones = jnp.where(rt >= cs, 1.0, 0.0).astype(BF16)
    w_tril = [jnp.where(rt >= cs, ws_ref[g, :c, :c], 0.0).astype(BF16) for g in range(MLP_GROUPS)]
    row_id = lax.broadcasted_iota(jnp.int32, (c, GLA_DK), 0)
    row_bit = [((row_id >> j) & 1) == 1 for j in range(n_levels)]
    q_scale = GLA_DK ** -0.5

    for ci in range(n_chunks):
        rows = slice(ci * c, (ci + 1) * c)

        b_all = sum(_dot(tri_ones, part) for part in _split_bf16(la_in[rows, :], 2))
        for h in range(GLA_HEADS):
            run_side_jobs()
            vs = slice(h * GLA_DV, (h + 1) * GLA_DV)
            b = b_all[:, h * GLA_DK:(h + 1) * GLA_DK]
            q = z_in[rows, Z_Q + h * GLA_DK:Z_Q + (h + 1) * GLA_DK]
            k = z_in[rows, Z_K + h * GLA_DK:Z_K + (h + 1) * GLA_DK]
            v = z_in[rows, Z_V + h * GLA_DV:Z_V + (h + 1) * GLA_DV]
            scores = jnp.where(level == 0, _dot_nt(q, k), 0.0)
            f = b
            for j in range(n_levels):
                half = 1 << j
                f_prev = pltpu.roll(f, half, 0)
                decay = jnp.exp(jnp.where(row_bit[j], b - f_prev, f - b)).astype(BF16)
                p = _dot_nt(q * decay, k * decay)
                scores = jnp.where(level == j + 1, p, scores)
                f = jnp.where(row_bit[j], f, pltpu.roll(f, c - half, 0))
            b_last = f
            st = s0_ref[ci, h].T if stream_per_chunk else st_scr[h]
            o = _dot(scores.astype(BF16), v) + _dot_nt(q * jnp.exp(b).astype(BF16), st.astype(BF16))
            k_dec = k * jnp.exp(b_last - b).astype(BF16)
            st_new = jnp.exp(b_last[0:1, :]) * st + _dot_tn(v, k_dec)
            if stream_per_chunk:
                sfin_ref[ci, h] = st_new.T
            else:
                st_scr[h] = st_new
            ms = jnp.mean(o * o, axis=-1, keepdims=True)
            inv = q_scale * lax.rsqrt(q_scale * q_scale * ms + EPS)
            g_a = z_in[rows, Z_GA + h * GLA_DV:Z_GA + (h + 1) * GLA_DV].astype(F32)
            o_a = o * inv * gng_ref[...] * _silu(g_a)
            o_ref[0, rows, vs] = o_a.astype(o_ref.dtype)

        for g in range(MLP_GROUPS):
            run_side_jobs()
            gs = slice(g * MLP_GC, (g + 1) * MLP_GC)
            vm = z_in[rows, Z_VM + g * MLP_GC:Z_VM + (g + 1) * MLP_GC].astype(F32)
            mu = jnp.mean(vm, axis=-1, keepdims=True)
            xc = vm - mu
            var = jnp.mean(xc * xc, axis=-1, keepdims=True)
            vn = xc * lax.rsqrt(var + EPS) * lng_ref[:, gs] + lnb_ref[:, gs]
            if vn_ref is not None:
                vn_ref[0, rows, gs] = vn
            sg = _dot(w_tril[g], vn.astype(BF16)) + bs_ref[g, :c, :]
            u = z_in[rows, Z_U + g * MLP_GC:Z_U + (g + 1) * MLP_GC].astype(F32)
            g_b = z_in[rows, Z_GB + g * MLP_GC:Z_GB + (g + 1) * MLP_GC].astype(F32)
            o_b = u * sg * _silu(g_b)
            o_ref[0, rows, GLA_WIDTH + g * MLP_GC:GLA_WIDTH + (g + 1) * MLP_GC] = o_b.astype(o_ref.dtype)


def _layer_in_kernel(*refs, chunk, emit_vn, n_cols, n_tiles, norm_input):
    refs = list(refs)
    x_ref = refs.pop(0)
    ng_ref = refs.pop(0) if norm_input else None
    w_ref, wlr_ref, wg_ref, bg_ref, s0_ref, gng_ref, lng_ref, lnb_ref, ws_ref, bs_ref = refs[:10]
    outs = refs[10:]
    o_ref, sfin_ref = outs[0], outs[1]
    vn_ref = outs[2] if emit_vn else None
    scratch = outs[3 if emit_vn else 2:]
    proj_w = (w_ref, wlr_ref, wg_ref, bg_ref)
    mix_p = (s0_ref, gng_ref, lng_ref, lnb_ref, ws_ref, bs_ref, o_ref, sfin_ref, vn_ref)

    def tile_input():
        x = x_ref[0]
        return _rmsnorm(x, ng_ref[...]).astype(BF16) if norm_input else x

    if n_tiles is None:
        z_scr, la_scr = scratch
        for job in _project_jobs(tile_input(), *proj_w, z_scr, la_scr, n_cols):
            job()
        _mix(z_scr, la_scr, *mix_p, None, chunk=chunk, stream_per_chunk=True)
        return

    z_scr, la_scr, st_scr = scratch
    step = pl.program_id(0)
    tile_in_stream = lax.rem(jnp.maximum(step - 1, 0), n_tiles)

    @pl.when(step == 0)
    def _():
        z_scr[1] = jnp.zeros(z_scr.shape[1:], z_scr.dtype)
        la_scr[1] = jnp.zeros(la_scr.shape[1:], la_scr.dtype)

    @pl.when(tile_in_stream == 0)
    def _():
        for h in range(GLA_HEADS):
            st_scr[h] = s0_ref[0, h].T

    for parity in range(2):
        @pl.when(lax.rem(step, 2) == parity)
        def _():
            nxt, cur = parity, 1 - parity
            _mix(z_scr.at[cur], la_scr.at[cur], *mix_p, st_scr, chunk=chunk, stream_per_chunk=False,
                 side_jobs=_project_jobs(tile_input(), *proj_w, z_scr.at[nxt], la_scr.at[nxt], n_cols))

    @pl.when(tile_in_stream == n_tiles - 1)
    def _():
        for h in range(GLA_HEADS):
            sfin_ref[0, h] = st_scr[h].T


def _layer_in(x, s0, s0_block, w, layer, *, chunk, emit_vn, pipelined, norm_input):
    nb, seq, _ = x.shape
    state_bytes = GLA_HEADS * GLA_DK * GLA_DV * 4
    n_cols = 256
    if pipelined:
        t = 256
        n_tiles = seq // t
        n_all = nb * n_tiles
        grid = (n_all + 1,)
        s_blk = 1
        mixed = lambda s: jnp.maximum(s - 1, 0)
        x_map = lambda s: (jnp.minimum(s, n_all - 1) // n_tiles, jnp.minimum(s, n_all - 1) % n_tiles, 0)
        row_map = lambda s: (mixed(s) // n_tiles, mixed(s) % n_tiles, 0)
        stream_of = lambda s: mixed(s) // n_tiles
        scratch_shapes = [pltpu.VMEM((2, t, Z_MAIN), BF16), pltpu.VMEM((2, t, GLA_KEY_WIDTH), F32),
                          pltpu.VMEM((GLA_HEADS, GLA_DV, GLA_DK), F32)]
        scratch_bytes = 2 * (t * Z_MAIN * 2 + t * GLA_KEY_WIDTH * 4) + state_bytes
        out_rows = (nb, seq)
    else:
        assert seq == chunk
        t = min(256, nb * seq)
        n_tiles = None
        s_blk = t // seq
        x = x.reshape(nb // s_blk, t, D_MODEL)
        grid = (nb // s_blk,)
        x_map = row_map = lambda s: (s, 0, 0)
        stream_of = lambda s: s
        scratch_shapes = [pltpu.VMEM((t, Z_MAIN), BF16), pltpu.VMEM((t, GLA_KEY_WIDTH), F32)]
        scratch_bytes = t * Z_MAIN * 2 + t * GLA_KEY_WIDTH * 4
        out_rows = (nb // s_blk, t)
    in_specs = [pl.BlockSpec((1, t, D_MODEL), x_map)]
    pipelined_bytes = t * D_MODEL * x.dtype.itemsize + t * D_MODEL * 2 + 2 * s_blk * state_bytes
    args = [x]
    if norm_input:
        in_specs.append(_layer_block((1, D_MODEL), layer))
        args.append(w["norm_g"])
    resident = (Z_MAIN * D_MODEL * 2 + LANES * D_MODEL * 2 + LANES * GLA_KEY_WIDTH * 4
                + MLP_GROUPS * MLP_CHUNK * (MLP_CHUNK + LANES) * 4 + scratch_bytes)
    state_out_spec = pl.BlockSpec((s_blk, GLA_HEADS, GLA_DK, GLA_DV), lambda s: (stream_of(s), 0, 0, 0))
    out_specs = [pl.BlockSpec((1, t, D_MODEL), row_map), state_out_spec]
    out_shape = [jax.ShapeDtypeStruct(out_rows + (D_MODEL,), BF16),
                 jax.ShapeDtypeStruct((nb, GLA_HEADS, GLA_DK, GLA_DV), F32)]
    if emit_vn:
        out_specs.append(pl.BlockSpec((1, t, MLP_WIDTH), row_map))
        out_shape.append(jax.ShapeDtypeStruct(out_rows + (MLP_WIDTH,), F32))
        pipelined_bytes += t * MLP_WIDTH * 4
    outs = pl.pallas_call(
        functools.partial(_layer_in_kernel, chunk=chunk, emit_vn=emit_vn, n_cols=n_cols, n_tiles=n_tiles,
                          norm_input=norm_input),
        grid=grid,
        in_specs=in_specs + [
            _layer_block((D_MODEL, Z_MAIN), layer),
            _layer_block((D_MODEL, LANES), layer),
            _layer_block((LANES, GLA_KEY_WIDTH), layer),
            _layer_block((1, GLA_KEY_WIDTH), layer),
            s0_block(s_blk, stream_of),
            _layer_block((1, GLA_DV), layer),
            _layer_block((1, MLP_WIDTH), layer),
            _layer_block((1, MLP_WIDTH), layer),
            _layer_block((MLP_GROUPS, MLP_CHUNK, MLP_CHUNK), layer),
            _layer_block((MLP_GROUPS, MLP_CHUNK, 1), layer),
        ],
        out_specs=out_specs,
        out_shape=out_shape,
        scratch_shapes=scratch_shapes,
        compiler_params=pltpu.CompilerParams(
            dimension_semantics=("arbitrary",),
            vmem_limit_bytes=_vmem_limit(pipelined_bytes, resident)),
        name="layer_in",
    )(*args, w["w_main"], w["w_lr"], w["w_g"], w["b_g"], s0, w["gla_norm_g"], w["ln_g"], w["ln_b"],
      w["w_s"], w["b_s"])
    return [outs[0].reshape(nb, seq, D_MODEL), outs[1]] + ([outs[2].reshape(nb, seq, MLP_WIDTH)] if emit_vn else [])


def _layer_out_kernel(o_ref, x_ref, w_ref, g_ref, *out_refs, last):
    y = x_ref[...] + _dot(o_ref[...], w_ref[...])
    yn = _rmsnorm(y, g_ref[...])
    if last:
        out_refs[0][...] = yn
    else:
        out_refs[0][...] = y
        out_refs[1][...] = yn.astype(out_refs[1].dtype)


def _layer_out(o2d, x2d, w, layer, *, last):
    rows = x2d.shape[0]
    tm = min(512, rows)
    row_spec = pl.BlockSpec((tm, D_MODEL), lambda i: (i, 0))
    out_specs = [row_spec]
    out_shape = [jax.ShapeDtypeStruct((rows, D_MODEL), F32)]
    pipelined = tm * D_MODEL * (2 + 4 + 4)
    if not last:
        out_specs.append(row_spec)
        out_shape.append(jax.ShapeDtypeStruct((rows, D_MODEL), BF16))
        pipelined += tm * D_MODEL * 2
    return pl.pallas_call(
        functools.partial(_layer_out_kernel, last=last),
        grid=(rows // tm,),
        in_specs=[row_spec, row_spec,
                  _layer_block((D_MODEL, D_MODEL), layer),
                  _layer_block((1, D_MODEL), layer + 1)],
        out_specs=out_specs,
        out_shape=out_shape,
        compiler_params=pltpu.CompilerParams(
            dimension_semantics=("arbitrary",),
            vmem_limit_bytes=_vmem_limit(pipelined, D_MODEL * D_MODEL * 2)),
        name="layer_out",
    )(o2d, x2d, w["w_out"], w["norm_g"])


def _layer(x, h, s0, s0_block, w, layer, *, emit_vn):
    nb, seq, _ = x.shape
    chunk = min(seq, MLP_CHUNK)
    last = layer == DEPTH - 1
    outs = _layer_in(x if h is None else h, s0, s0_block, w, layer, chunk=chunk, emit_vn=emit_vn,
                     pipelined=seq > chunk, norm_input=h is None)
    o, s_fin = outs[0], outs[1]
    ys = _layer_out(o.reshape(nb * seq, D_MODEL), x.reshape(nb * seq, D_MODEL), w, layer, last=last)
    y = ys[0].reshape(nb, seq, D_MODEL)
    h_next = None if last else ys[1].reshape(nb, seq, D_MODEL)
    return y, h_next, s_fin, (outs[2] if emit_vn else None)


def kernel(x_prompt, x_sample, state_gla, w_in, w_gate_up, b_gate, w_s, b_s, norm_g, gla_norm_g,
           mlp_ln_g, mlp_ln_b, w_out, final_norm_g):
    w_lr = jnp.pad(jnp.tile(w_in[:, :, Z_LR:Z_LR + GATE_RANK], (1, 1, 3)),
                   ((0, 0), (0, 0), (0, LANES - 3 * GATE_RANK)))
    w = dict(
        w_main=_prep_w_in(jnp.swapaxes(w_in, 1, 2)), w_lr=w_lr.astype(BF16),
        w_g=jnp.pad(jnp.tile(w_gate_up, (1, 3, 1)), ((0, 0), (0, LANES - 3 * GATE_RANK), (0, 0))),
        b_g=b_gate[:, None, :],
        gla_norm_g=gla_norm_g[:, None, :], ln_g=mlp_ln_g[:, None, :], ln_b=mlp_ln_b[:, None, :],
        w_s=w_s, b_s=b_s[..., None],
        w_out=w_out.astype(BF16),
        norm_g=jnp.concatenate([norm_g, final_norm_g[None]], axis=0)[:, None, :],
    )
    zero_state = jnp.zeros((x_prompt.shape[0], GLA_HEADS, GLA_DK, GLA_DV), F32)

    def prompt_state_block(n_streams, stream_of):
        return pl.BlockSpec((n_streams, GLA_HEADS, GLA_DK, GLA_DV), lambda s: (stream_of(s), 0, 0, 0))

    yp, ys = x_prompt, x_sample
    hp = hs = None
    gla_p, gla_s, v_s = [], [], []
    for l in range(DEPTH):
        def sample_state_block(n_streams, stream_of, l=l):
            return pl.BlockSpec((None, n_streams, GLA_HEADS, GLA_DK, GLA_DV),
                                lambda s: (l, stream_of(s), 0, 0, 0))

        yp, hp, sp, _ = _layer(yp, hp, zero_state, prompt_state_block, w, l, emit_vn=False)
        ys, hs, ss, vs = _layer(ys, hs, state_gla, sample_state_block, w, l, emit_vn=True)
        gla_p.append(sp)
        gla_s.append(ss)
        v_s.append(vs)
    return (yp, ys, jnp.stack(gla_p), jnp.stack(gla_s), jnp.stack(v_s))
```

```python
import functools

import jax
import jax.numpy as jnp
from jax import lax
from jax.experimental import pallas as pl
from jax.experimental.pallas import tpu as pltpu

D_MODEL = 2048
DEPTH = 4
GLA_HEADS = 4
GLA_DK = 128
GLA_DV = 256
GLA_KEY_WIDTH = GLA_HEADS * GLA_DK
GLA_WIDTH = GLA_HEADS * GLA_DV
GATE_RANK = 16
GATE_TEMP = 16.0
MLP_GROUPS = 4
MLP_GC = 256
MLP_WIDTH = MLP_GROUPS * MLP_GC
MLP_CHUNK = 128
EPS = 1e-6

Z_Q = 0
Z_K = Z_Q + GLA_KEY_WIDTH
Z_V = Z_K + GLA_KEY_WIDTH
Z_GA = Z_V + GLA_WIDTH
Z_LR = Z_GA + GLA_WIDTH
Z_U = Z_GA + GLA_WIDTH
Z_VM = Z_U + MLP_WIDTH
Z_GB = Z_VM + MLP_WIDTH
Z_MAIN = Z_GB + MLP_WIDTH
D_IN = Z_MAIN + GATE_RANK

LANES = 128
V7X_VMEM_BYTES = 64 * 1024 * 1024
VMEM_INTERNAL_BYTES = 12 * 1024 * 1024

F32 = jnp.float32
BF16 = jnp.bfloat16


def _vmem_limit(pipelined_bytes, resident_bytes=0):
    need = 2 * pipelined_bytes + resident_bytes + VMEM_INTERNAL_BYTES
    return min(need, V7X_VMEM_BYTES - 6 * 1024 * 1024)


def _dot(a, b):
    return jnp.dot(a, b, preferred_element_type=F32)


def _dot_nt(a, b):
    return lax.dot_general(a, b, (((1,), (1,)), ((), ())), preferred_element_type=F32)


def _dot_tn(a, b):
    return lax.dot_general(a, b, (((0,), (0,)), ((), ())), preferred_element_type=F32)


def _split_bf16(x, parts):
    out = []
    r = x
    for _ in range(parts - 1):
        p = r.astype(BF16)
        out.append(p)
        r = r - p.astype(F32)
    out.append(r.astype(BF16))
    return out


def _silu(x):
    hx = 0.5 * x
    return hx + hx * jnp.tanh(hx)


def _rmsnorm(x, g):
    ms = jnp.mean(x * x, axis=-1, keepdims=True)
    return x * lax.rsqrt(ms + EPS) * g


def _resident(block_shape, index_map):
    return pl.BlockSpec(block_shape, index_map, pipeline_mode=pl.Buffered(1))


def _layer_block(shape, layer):
    zeros = (0,) * len(shape)
    return _resident((None,) + tuple(shape), lambda *_: (layer,) + zeros)


PREP_ROWS = 512


def _prep_w_in_kernel(w_ref, o_ref):
    o_ref[...] = w_ref[0].T.astype(o_ref.dtype)


def _prep_w_in(w_in_t):
    n_layers = w_in_t.shape[0]
    blk = PREP_ROWS * D_MODEL

    def src_rows(l, m):
        past_gate = (m * PREP_ROWS >= Z_LR).astype(jnp.int32)
        return (l, (m * (PREP_ROWS // GATE_RANK) + past_gate) * GATE_RANK, 0)

    return pl.pallas_call(
        _prep_w_in_kernel,
        grid=(n_layers, Z_MAIN // PREP_ROWS),
        in_specs=[pl.BlockSpec((pl.Element(1), pl.Element(PREP_ROWS), pl.Element(D_MODEL)), src_rows)],
        out_specs=pl.BlockSpec((None, D_MODEL, PREP_ROWS), lambda l, m: (l, 0, m)),
        out_shape=jax.ShapeDtypeStruct((n_layers, D_MODEL, Z_MAIN), BF16),
        compiler_params=pltpu.CompilerParams(
            dimension_semantics=("arbitrary", "arbitrary"),
            vmem_limit_bytes=_vmem_limit(blk * 4 + blk * 2)),
        name="prep_w_in",
    )(w_in_t)


def _project_jobs(hx, w_ref, wlr_ref, wg_ref, bg_ref, z_out, la_out, n_cols):
    def cols_job(col):
        z_out[:, col:col + n_cols] = _dot(hx, w_ref[:, col:col + n_cols]).astype(BF16)

    def gate_job():
        lr = _dot(hx, wlr_ref[...])
        lane = lax.broadcasted_iota(jnp.int32, lr.shape, 1)
        lr_lo = lr - lr.astype(BF16).astype(F32)
        lhs = jnp.where((lane >= GATE_RANK) & (lane < 2 * GATE_RANK), lr_lo, lr).astype(BF16)
        wg = wg_ref[...]
        row = lax.broadcasted_iota(jnp.int32, wg.shape, 0)
        wg_lo = wg - wg.astype(BF16).astype(F32)
        rhs = jnp.where(row >= 2 * GATE_RANK, wg_lo, wg).astype(BF16)
        xg = _dot(lhs, rhs) + bg_ref[...]
        log_sig = jnp.minimum(xg, 0.0) - jnp.log(1.0 + jnp.exp(-jnp.abs(xg)))
        la_out[...] = log_sig * (1.0 / GATE_TEMP)

    return [functools.partial(cols_job, col) for col in range(0, Z_MAIN, n_cols)] + [gate_job]


def _mix(z_in, la_in, s0_ref, gng_ref, lng_ref, lnb_ref, ws_ref, bs_ref, o_ref, sfin_ref, vn_ref, st_scr,
         *, chunk, stream_per_chunk, side_jobs=()):
    c = chunk
    n_levels = c.bit_length() - 1
    n_chunks = z_in.shape[0] // c
    side_jobs = list(side_jobs)
    n_side = max(len(side_jobs), 1)
    n_units = n_chunks * (GLA_HEADS + MLP_GROUPS)
    units_started = 0

    def run_side_jobs():
        nonlocal units_started
        units_started += 1
        while side_jobs and len(side_jobs) * n_units > (n_units - units_started) * n_side:
            side_jobs.pop(0)()

    rt = lax.broadcasted_iota(jnp.int32, (c, c), 0)
    cs = lax.broadcasted_iota(jnp.int32, (c, c), 1)
    differ = rt ^ cs
    top_bit = jnp.zeros((c, c), jnp.int32)
    for j in range(n_levels):
        top_bit = top_bit + jnp.where(differ >= (1 << j), 1, 0)
    level = jnp.where(rt >= cs, top_bit, -1)
    tri_ones = jnp.where(rt >= cs, 1.0, 0.0).astype(BF16)
    tri_twice = jnp.concatenate([tri_ones, tri_ones], axis=1)
    w_tril = [jnp.where(rt >= cs, ws_ref[g, :c, :c], 0.0).astype(BF16) for g in range(MLP_GROUPS)]
    row_id = lax.broadcasted_iota(jnp.int32, (c, GLA_DK), 0)
    row_bit = [((row_id >> j) & 1) == 1 for j in range(n_levels)]
    q_scale = GLA_DK ** -0.5

    for ci in range(n_chunks):
        rows = slice(ci * c, (ci + 1) * c)

        b_all = _dot(tri_twice, jnp.concatenate(_split_bf16(la_in[rows, :], 2), axis=0))

        def gla_unit(h, rows=rows, ci=ci, b_all=b_all):
            run_side_jobs()
            vs = slice(h * GLA_DV, (h + 1) * GLA_DV)
            b = b_all[:, h * GLA_DK:(h + 1) * GLA_DK]
            q = z_in[rows, Z_Q + h * GLA_DK:Z_Q + (h + 1) * GLA_DK]
            k = z_in[rows, Z_K + h * GLA_DK:Z_K + (h + 1) * GLA_DK]
            v = z_in[rows, Z_V + h * GLA_DV:Z_V + (h + 1) * GLA_DV]
            scores = jnp.where(level == 0, _dot_nt(q, k), 0.0)
            f = b
            for j in range(n_levels):
                half = 1 << j
                f_prev = pltpu.roll(f, half, 0)
                decay = jnp.exp(jnp.where(row_bit[j], b - f_prev, f - b)).astype(BF16)
                p = _dot_nt(q * decay, k * decay)
                scores = jnp.where(level == j + 1, p, scores)
                f = jnp.where(row_bit[j], f, pltpu.roll(f, c - half, 0))
            b_last = f
            st = s0_ref[ci, h].T if stream_per_chunk else st_scr[h]
            o = _dot(scores.astype(BF16), v) + _dot_nt(q * jnp.exp(b).astype(BF16), st.astype(BF16))
            k_dec = k * jnp.exp(b_last - b).astype(BF16)
            st_new = jnp.exp(b_last[0:1, :]) * st + _dot_tn(v, k_dec)
            if stream_per_chunk:
                sfin_ref[ci, h] = st_new.T
            else:
                st_scr[h] = st_new
            ms = jnp.mean(o * o, axis=-1, keepdims=True)
            inv = q_scale * lax.rsqrt(q_scale * q_scale * ms + EPS)
            g_a = z_in[rows, Z_GA + h * GLA_DV:Z_GA + (h + 1) * GLA_DV].astype(F32)
            o_a = o * inv * gng_ref[...] * _silu(g_a)
            o_ref[0, rows, vs] = o_a.astype(o_ref.dtype)

        def mlp_unit(g, rows=rows):
            run_side_jobs()
            gs = slice(g * MLP_GC, (g + 1) * MLP_GC)
            vm = z_in[rows, Z_VM + g * MLP_GC:Z_VM + (g + 1) * MLP_GC].astype(F32)
            mu = jnp.mean(vm, axis=-1, keepdims=True)
            xc = vm - mu
            var = jnp.mean(xc * xc, axis=-1, keepdims=True)
            vn = xc * lax.rsqrt(var + EPS) * lng_ref[:, gs] + lnb_ref[:, gs]
            if vn_ref is not None:
                vn_ref[0, rows, gs] = vn
            sg = _dot(w_tril[g], vn.astype(BF16)) + bs_ref[g, :c, :]
            u = z_in[rows, Z_U + g * MLP_GC:Z_U + (g + 1) * MLP_GC].astype(F32)
            g_b = z_in[rows, Z_GB + g * MLP_GC:Z_GB + (g + 1) * MLP_GC].astype(F32)
            o_b = u * sg * _silu(g_b)
            o_ref[0, rows, GLA_WIDTH + g * MLP_GC:GLA_WIDTH + (g + 1) * MLP_GC] = o_b.astype(o_ref.dtype)

        for i in range(max(GLA_HEADS, MLP_GROUPS)):
            if i < GLA_HEADS:
                gla_unit(i)
            if i < MLP_GROUPS:
                mlp_unit(i)


def _layer_in_kernel(*refs, chunk, emit_vn, n_cols, n_tiles, norm_input):
    refs = list(refs)
    x_ref = refs.pop(0)
    ng_ref = refs.pop(0) if norm_input else None
    w_ref, wlr_ref, wg_ref, bg_ref, s0_ref, gng_ref, lng_ref, lnb_ref, ws_ref, bs_ref = refs[:10]
    outs = refs[10:]
    o_ref, sfin_ref = outs[0], outs[1]
    vn_ref = outs[2] if emit_vn else None
    scratch = outs[3 if emit_vn else 2:]
    proj_w = (w_ref, wlr_ref, wg_ref, bg_ref)
    mix_p = (s0_ref, gng_ref, lng_ref, lnb_ref, ws_ref, bs_ref, o_ref, sfin_ref, vn_ref)

    def tile_input():
        x = x_ref[0]
        return _rmsnorm(x, ng_ref[...]).astype(BF16) if norm_input else x

    if n_tiles is None:
        z_scr, la_scr = scratch
        for job in _project_jobs(tile_input(), *proj_w, z_scr, la_scr, n_cols):
            job()
        _mix(z_scr, la_scr, *mix_p, None, chunk=chunk, stream_per_chunk=True)
        return

    z_scr, la_scr, st_scr = scratch
    step = pl.program_id(0)
    tile_in_stream = lax.rem(jnp.maximum(step - 1, 0), n_tiles)

    @pl.when(step == 0)
    def _():
        z_scr[1] = jnp.zeros(z_scr.shape[1:], z_scr.dtype)
        la_scr[1] = jnp.zeros(la_scr.shape[1:], la_scr.dtype)

    @pl.when(tile_in_stream == 0)
    def _():
        for h in range(GLA_HEADS):
            st_scr[h] = s0_ref[0, h].T

    for parity in range(2):
        @pl.when(lax.rem(step, 2) == parity)
        def _():
            nxt, cur = parity, 1 - parity
            _mix(z_scr.at[cur], la_scr.at[cur], *mix_p, st_scr, chunk=chunk, stream_per_chunk=False,
                 side_jobs=_project_jobs(tile_input(), *proj_w, z_scr.at[nxt], la_scr.at[nxt], n_cols))

    @pl.when(tile_in_stream == n_tiles - 1)
    def _():
        for h in range(GLA_HEADS):
            sfin_ref[0, h] = st_scr[h].T


def _layer_in(x, s0, s0_block, w, layer, *, chunk, emit_vn, pipelined, norm_input):
    nb, seq, _ = x.shape
    state_bytes = GLA_HEADS * GLA_DK * GLA_DV * 4
    n_cols = 256
    if pipelined:
        t = 256
        n_tiles = seq // t
        n_all = nb * n_tiles
        grid = (n_all + 1,)
        s_blk = 1
        mixed = lambda s: jnp.maximum(s - 1, 0)
        x_map = lambda s: (jnp.minimum(s, n_all - 1) // n_tiles, jnp.minimum(s, n_all - 1) % n_tiles, 0)
        row_map = lambda s: (mixed(s) // n_tiles, mixed(s) % n_tiles, 0)
        stream_of = lambda s: mixed(s) // n_tiles
        scratch_shapes = [pltpu.VMEM((2, t, Z_MAIN), BF16), pltpu.VMEM((2, t, GLA_KEY_WIDTH), F32),
                          pltpu.VMEM((GLA_HEADS, GLA_DV, GLA_DK), F32)]
        scratch_bytes = 2 * (t * Z_MAIN * 2 + t * GLA_KEY_WIDTH * 4) + state_bytes
        out_rows = (nb, seq)
    else:
        assert seq == chunk
        t = min(256, nb * seq)
        n_tiles = None
        s_blk = t // seq
        x = x.reshape(nb // s_blk, t, D_MODEL)
        grid = (nb // s_blk,)
        x_map = row_map = lambda s: (s, 0, 0)
        stream_of = lambda s: s
        scratch_shapes = [pltpu.VMEM((t, Z_MAIN), BF16), pltpu.VMEM((t, GLA_KEY_WIDTH), F32)]
        scratch_bytes = t * Z_MAIN * 2 + t * GLA_KEY_WIDTH * 4
        out_rows = (nb // s_blk, t)
    in_specs = [pl.BlockSpec((1, t, D_MODEL), x_map)]
    pipelined_bytes = t * D_MODEL * x.dtype.itemsize + t * D_MODEL * 2 + 2 * s_blk * state_bytes
    args = [x]
    if norm_input:
        in_specs.append(_layer_block((1, D_MODEL), layer))
        args.append(w["norm_g"])
    resident = (Z_MAIN * D_MODEL * 2 + LANES * D_MODEL * 2 + LANES * GLA_KEY_WIDTH * 4
                + MLP_GROUPS * MLP_CHUNK * (MLP_CHUNK + LANES) * 4 + scratch_bytes)
    state_out_spec = pl.BlockSpec((s_blk, GLA_HEADS, GLA_DK, GLA_DV), lambda s: (stream_of(s), 0, 0, 0))
    out_specs = [pl.BlockSpec((1, t, D_MODEL), row_map), state_out_spec]
    out_shape = [jax.ShapeDtypeStruct(out_rows + (D_MODEL,), BF16),
                 jax.ShapeDtypeStruct((nb, GLA_HEADS, GLA_DK, GLA_DV), F32)]
    if emit_vn:
        out_specs.append(pl.BlockSpec((1, t, MLP_WIDTH), row_map))
        out_shape.append(jax.ShapeDtypeStruct(out_rows + (MLP_WIDTH,), F32))
        pipelined_bytes += t * MLP_WIDTH * 4
    outs = pl.pallas_call(
        functools.partial(_layer_in_kernel, chunk=chunk, emit_vn=emit_vn, n_cols=n_cols, n_tiles=n_tiles,
                          norm_input=norm_input),
        grid=grid,
        in_specs=in_specs + [
            _layer_block((D_MODEL, Z_MAIN), layer),
            _layer_block((D_MODEL, LANES), layer),
            _layer_block((LANES, GLA_KEY_WIDTH), layer),
            _layer_block((1, GLA_KEY_WIDTH), layer),
            s0_block(s_blk, stream_of),
            _layer_block((1, GLA_DV), layer),
            _layer_block((1, MLP_WIDTH), layer),
            _layer_block((1, MLP_WIDTH), layer),
            _layer_block((MLP_GROUPS, MLP_CHUNK, MLP_CHUNK), layer),
            _layer_block((MLP_GROUPS, MLP_CHUNK, 1), layer),
        ],
        out_specs=out_specs,
        out_shape=out_shape,
        scratch_shapes=scratch_shapes,
        compiler_params=pltpu.CompilerParams(
            dimension_semantics=("arbitrary",),
            vmem_limit_bytes=_vmem_limit(pipelined_bytes, resident)),
        name="layer_in",
    )(*args, w["w_main"], w["w_lr"], w["w_g"], w["b_g"], s0, w["gla_norm_g"], w["ln_g"], w["ln_b"],
      w["w_s"], w["b_s"])
    return [outs[0].reshape(nb, seq, D_MODEL), outs[1]] + ([outs[2].reshape(nb, seq, MLP_WIDTH)] if emit_vn else [])


def _layer_out_kernel(o_ref, x_ref, w_ref, g_ref, *out_refs, last):
    y = x_ref[...] + _dot(o_ref[...], w_ref[...])
    yn = _rmsnorm(y, g_ref[...])
    if last:
        out_refs[0][...] = yn
    else:
        out_refs[0][...] = y
        out_refs[1][...] = yn.astype(out_refs[1].dtype)


def _layer_out(o2d, x2d, w, layer, *, last):
    rows = x2d.shape[0]
    tm = min(512, rows)
    row_spec = pl.BlockSpec((tm, D_MODEL), lambda i: (i, 0))
    out_specs = [row_spec]
    out_shape = [jax.ShapeDtypeStruct((rows, D_MODEL), F32)]
    pipelined = tm * D_MODEL * (2 + 4 + 4)
    if not last:
        out_specs.append(row_spec)
        out_shape.append(jax.ShapeDtypeStruct((rows, D_MODEL), BF16))
        pipelined += tm * D_MODEL * 2
    return pl.pallas_call(
        functools.partial(_layer_out_kernel, last=last),
        grid=(rows // tm,),
        in_specs=[row_spec, row_spec,
                  _layer_block((D_MODEL, D_MODEL), layer),
                  _layer_block((1, D_MODEL), layer + 1)],
        out_specs=out_specs,
        out_shape=out_shape,
        compiler_params=pltpu.CompilerParams(
            dimension_semantics=("arbitrary",),
            vmem_limit_bytes=_vmem_limit(pipelined, D_MODEL * D_MODEL * 2)),
        name="layer_out",
    )(o2d, x2d, w["w_out"], w["norm_g"])


def _layer(x, h, s0, s0_block, w, layer, *, emit_vn):
    nb, seq, _ = x.shape
    chunk = min(seq, MLP_CHUNK)
    last = layer == DEPTH - 1
    outs = _layer_in(x if h is None else h, s0, s0_block, w, layer, chunk=chunk, emit_vn=emit_vn,
                     pipelined=seq > chunk, norm_input=h is None)
    o, s_fin = outs[0], outs[1]
    ys = _layer_out(o.reshape(nb * seq, D_MODEL), x.reshape(nb * seq, D_MODEL), w, layer, last=last)
    y = ys[0].reshape(nb, seq, D_MODEL)
    h_next = None if last else ys[1].reshape(nb, seq, D_MODEL)
    return y, h_next, s_fin, (outs[2] if emit_vn else None)


def kernel(x_prompt, x_sample, state_gla, w_in, w_gate_up, b_gate, w_s, b_s, norm_g, gla_norm_g,
           mlp_ln_g, mlp_ln_b, w_out, final_norm_g):
    w_lr = jnp.pad(jnp.tile(w_in[:, :, Z_LR:Z_LR + GATE_RANK], (1, 1, 3)),
                   ((0, 0), (0, 0), (0, LANES - 3 * GATE_RANK)))
    w = dict(
        w_main=_prep_w_in(jnp.swapaxes(w_in, 1, 2)), w_lr=w_lr.astype(BF16),
        w_g=jnp.pad(jnp.tile(w_gate_up, (1, 3, 1)), ((0, 0), (0, LANES - 3 * GATE_RANK), (0, 0))),
        b_g=b_gate[:, None, :],
        gla_norm_g=gla_norm_g[:, None, :], ln_g=mlp_ln_g[:, None, :], ln_b=mlp_ln_b[:, None, :],
        w_s=w_s, b_s=b_s[..., None],
        w_out=w_out.astype(BF16),
        norm_g=jnp.concatenate([norm_g, final_norm_g[None]], axis=0)[:, None, :],
    )
    zero_state = jnp.zeros((x_prompt.shape[0], GLA_HEADS, GLA_DK, GLA_DV), F32)

    def prompt_state_block(n_streams, stream_of):
        return pl.BlockSpec((n_streams, GLA_HEADS, GLA_DK, GLA_DV), lambda s: (stream_of(s), 0, 0, 0))

    yp, ys = x_prompt, x_sample
    hp = hs = None
    gla_p, gla_s, v_s = [], [], []
    for l in range(DEPTH):
        def sample_state_block(n_streams, stream_of, l=l):
            return pl.BlockSpec((None, n_streams, GLA_HEADS, GLA_DK, GLA_DV),
                                lambda s: (l, stream_of(s), 0, 0, 0))

        yp, hp, sp, _ = _layer(yp, hp, zero_state, prompt_state_block, w, l, emit_vn=False)
        ys, hs, ss, vs = _layer(ys, hs, state_gla, sample_state_block, w, l, emit_vn=True)
        gla_p.append(sp)
        gla_s.append(ss)
        v_s.append(vs)
    return (yp, ys, jnp.stack(gla_p), jnp.stack(gla_s), jnp.stack(v_s))
```
